```python
import math
import jax, jax.numpy as jnp
from jax import lax
import numpy as np

D_MODEL = 4096
BATCH = 4
SEQ = 2048
DEPTH = 1

HEAD_DIM = 128
N_HEADS_A = 16
N_HEADS_B = 16
WA = N_HEADS_A * HEAD_DIM
WB = N_HEADS_B * HEAD_DIM
IDX_HEADS = 16
IDX_DIM = 64
TOPK_MAX = 256
Q_BLOCK = 128
N_BUCKETS = 32
MAX_DISTANCE = 128
D_FF = 11008
N_MOD = 9
RMS_EPS = 1e-6
FORGET_BIAS_MEAN = 2.0
D_IN = WA + 2 * HEAD_DIM + IDX_HEADS * IDX_DIM + IDX_DIM + IDX_HEADS + 3 * WB + N_HEADS_B + 2 * D_MODEL

kernel_name = 'hybrid_dsa_fox_macaron_adaln_block'


def _split_points():
    sizes = (WA, HEAD_DIM, HEAD_DIM, IDX_HEADS * IDX_DIM, IDX_DIM, IDX_HEADS,
             WB, WB, WB, N_HEADS_B, D_MODEL, D_MODEL)
    return tuple(int(v) for v in np.cumsum(sizes)[:-1])


def rms_norm(x, g):
    xf = x.astype(jnp.float32)
    y = xf * lax.rsqrt(jnp.mean(xf * xf, axis=-1, keepdims=True) + RMS_EPS)
    return (y * g.astype(jnp.float32)).astype(x.dtype)


def modulate(h, shift, scale):
    return h * (1.0 + scale[:, None, :]) + shift[:, None, :]


def swiglu(h, w_in, w_out):
    a, b = jnp.split(h @ w_in, 2, axis=-1)
    return (jax.nn.silu(a) * b) @ w_out


def t5_bucket(dist):
    n = jnp.maximum(dist, 0)
    max_exact = N_BUCKETS // 2
    nf = jnp.maximum(n, max_exact).astype(jnp.float32)
    large = max_exact + (jnp.log(nf / max_exact) / math.log(MAX_DISTANCE / max_exact)
                         * (N_BUCKETS - max_exact)).astype(jnp.int32)
    large = jnp.minimum(large, N_BUCKETS - 1)
    return jnp.where(n < max_exact, n, large)


def _to_blocks(a):
    B, L = a.shape[:2]
    return jnp.moveaxis(a.reshape(B, L // Q_BLOCK, Q_BLOCK, *a.shape[2:]), 1, 0)


def _from_blocks(a):
    nb, B, q = a.shape[:3]
    return jnp.moveaxis(a, 0, 1).reshape(B, nb * q, *a.shape[3:])


_gather_rows = jax.vmap(lambda table, idx: table[idx])


def dsa_attention(q, k, v, q_idx, k_idx, w_idx, rel_bias):
    B, L = q.shape[:2]
    topk = min(TOPK_MAX, L // 4)
    pos = jnp.arange(L, dtype=jnp.int32)
    scale = HEAD_DIM ** -0.5
    w_idx = w_idx * (IDX_HEADS ** -0.5 * IDX_DIM ** -0.5)

    def block(args):
        qb, qib, wb, tb = args
        s = jax.nn.relu(jnp.einsum('bqhd,bsd->bqhs', qib, k_idx).astype(jnp.float32))
        score = jnp.einsum('bqhs,bqh->bqs', s, wb.astype(jnp.float32))
        causal = pos[None, :] <= tb[:, None]
        score = jnp.where(causal[None], score, -jnp.inf)
        _, idx = lax.top_k(score, topk)
        k_sel = _gather_rows(k, idx)
        v_sel = _gather_rows(v, idx)
        dist = tb[None, :, None] - idx
        bias = jnp.moveaxis(rel_bias[t5_bucket(dist)], -1, 1)
        logits = (jnp.einsum('bqhd,bqkd->bhqk', qb, k_sel).astype(jnp.float32) * scale
                  + bias.astype(jnp.float32))
        logits = jnp.where((dist >= 0)[:, None], logits, -jnp.inf)
        p = jax.nn.softmax(logits, axis=-1).astype(v.dtype)
        return jnp.einsum('bhqk,bqkd->bqhd', p, v_sel)

    out = lax.map(block, (_to_blocks(q), _to_blocks(q_idx), _to_blocks(w_idx),
                          pos.reshape(-1, Q_BLOCK)))
    return _from_blocks(out).reshape(B, L, WA)


def forgetting_attention(q, k, v, f_logit):
    B, L = q.shape[:2]
    pos = jnp.arange(L, dtype=jnp.int32)
    scale = HEAD_DIM ** -0.5
    F = lax.cumsum(jax.nn.log_sigmoid(f_logit.astype(jnp.float32)), axis=1)
    F_key = jnp.moveaxis(F, -1, 1)[:, :, None, :]

    def block(args):
        qb, Fq, tb = args
        logits = jnp.einsum('bqhd,bshd->bhqs', qb, k).astype(jnp.float32) * scale
        logits = logits + jnp.moveaxis(Fq, -1, 1)[..., None] - F_key
        causal = pos[None, :] <= tb[:, None]
        logits = jnp.where(causal[None, None], logits, -jnp.inf)
        p = jax.nn.softmax(logits, axis=-1).astype(v.dtype)
        return jnp.einsum('bhqs,bshd->bqhd', p, v)

    out = lax.map(block, (_to_blocks(q), _to_blocks(F), pos.reshape(-1, Q_BLOCK)))
    return _from_blocks(out).reshape(B, L, WB)


def hybrid_mixer(h, w_in, b_forget, rel_bias, w_up_a, w_up_b, w_o):
    B, L, _ = h.shape
    proj = h @ w_in
    (q_a, k_a, v_a, q_i, k_i, w_i, q_b, k_b, v_b, f_b, gate_a, gate_b) = jnp.split(
        proj, _split_points(), axis=-1)
    o_a = dsa_attention(q_a.reshape(B, L, N_HEADS_A, HEAD_DIM), k_a, v_a,
                        q_i.reshape(B, L, IDX_HEADS, IDX_DIM), k_i, w_i, rel_bias)
    o_b = forgetting_attention(q_b.reshape(B, L, N_HEADS_B, HEAD_DIM),
                               k_b.reshape(B, L, N_HEADS_B, HEAD_DIM),
                               v_b.reshape(B, L, N_HEADS_B, HEAD_DIM),
                               f_b + b_forget)
    merged = jax.nn.sigmoid(gate_a) * (o_a @ w_up_a) + jax.nn.sigmoid(gate_b) * (o_b @ w_up_b)
    return merged @ w_o


def setup_inputs(seed: int = 0) -> dict:
    key = jax.random.key(seed)
    ks = jax.random.split(key, 18)
    f32 = jnp.float32

    def dense(k, shape, fan_in):
        return jax.random.normal(k, shape, f32) * fan_in ** -0.5

    def gain(k, shape):
        return 1.0 + 0.02 * jax.random.normal(k, shape, f32)

    return {
        'x': jax.random.normal(ks[0], (BATCH, SEQ, D_MODEL), f32),
        'c': jax.random.normal(ks[1], (BATCH, D_MODEL), f32),
        'w_ada': dense(ks[2], (DEPTH, D_MODEL, N_MOD * D_MODEL), D_MODEL),
        'b_ada': 0.02 * jax.random.normal(ks[3], (DEPTH, N_MOD * D_MODEL), f32),
        'g_ffn1': gain(ks[4], (DEPTH, D_MODEL)),
        'ffn1_w_in': dense(ks[5], (DEPTH, D_MODEL, 2 * D_FF), D_MODEL),
        'ffn1_w_out': dense(ks[6], (DEPTH, D_FF, D_MODEL), D_FF),
        'g_mix': gain(ks[7], (DEPTH, D_MODEL)),
        'w_in': dense(ks[8], (DEPTH, D_MODEL, D_IN), D_MODEL),
        'b_forget': FORGET_BIAS_MEAN + 0.5 * jax.random.normal(ks[9], (DEPTH, N_HEADS_B), f32),
        'rel_bias': 0.5 * jax.random.normal(ks[10], (N_BUCKETS, N_HEADS_A), f32),
        'w_up_a': dense(ks[11], (DEPTH, WA, D_MODEL), WA),
        'w_up_b': dense(ks[12], (DEPTH, WB, D_MODEL), WB),
        'w_o': dense(ks[13], (DEPTH, D_MODEL, D_MODEL), D_MODEL),
        'g_ffn2': gain(ks[14], (DEPTH, D_MODEL)),
        'ffn2_w_in': dense(ks[15], (DEPTH, D_MODEL, 2 * D_FF), D_MODEL),
        'ffn2_w_out': dense(ks[16], (DEPTH, D_FF, D_MODEL), D_FF),
        'g_final': gain(ks[17], (D_MODEL,)),
    }


def reference(x, c, w_ada, b_ada, g_ffn1, ffn1_w_in, ffn1_w_out, g_mix, w_in, b_forget,
              rel_bias, w_up_a, w_up_b, w_o, g_ffn2, ffn2_w_in, ffn2_w_out, g_final):
    c_act = jax.nn.silu(c)
    for l in range(DEPTH):
        mod = c_act @ w_ada[l] + b_ada[l]
        sh1, sc1, gt1, sh2, sc2, gt2, sh3, sc3, gt3 = jnp.split(mod, N_MOD, axis=-1)
        h = modulate(rms_norm(x, g_ffn1[l]), sh1, sc1)
        x = x + 0.5 * gt1[:, None, :] * swiglu(h, ffn1_w_in[l], ffn1_w_out[l])
        h = modulate(rms_norm(x, g_mix[l]), sh2, sc2)
        x = x + gt2[:, None, :] * hybrid_mixer(h, w_in[l], b_forget[l], rel_bias,
                                               w_up_a[l], w_up_b[l], w_o[l])
        h = modulate(rms_norm(x, g_ffn2[l]), sh3, sc3)
        x = x + 0.5 * gt3[:, None, :] * swiglu(h, ffn2_w_in[l], ffn2_w_out[l])
    return rms_norm(x, g_final)
```

```python
import functools
import math

import numpy as np
import jax
import jax.numpy as jnp
from jax import lax
from jax.experimental import pallas as pl
from jax.experimental.pallas import tpu as pltpu

F32 = jnp.float32
BF16 = jnp.bfloat16

D_MODEL = 4096
HEAD_DIM = 128
N_HEADS = 16
W_ATT = N_HEADS * HEAD_DIM
IDX_HEADS = 16
IDX_DIM = 64
TOPK_MAX = 256
N_BUCKETS = 32
MAX_DISTANCE = 128
D_FF = 11008
N_MOD = 9
RMS_EPS = 1e-6
ATT_SCALE = HEAD_DIM ** -0.5
IDX_SCALE = IDX_HEADS ** -0.5 * IDX_DIM ** -0.5

LANE = 128
MIB = 1024 * 1024
NEG_BIG = -1e30
INT_MIN = -(2 ** 31)
KEY_NEG_INF = (0xFF800000 ^ 0x7FFFFFFF) - 2 ** 32

OFF_GATE_A = 0
OFF_GATE_B = OFF_GATE_A + D_MODEL
OFF_QA = OFF_GATE_B + D_MODEL
OFF_QB = OFF_QA + W_ATT
OFF_KB = OFF_QB + W_ATT
OFF_VB = OFF_KB + W_ATT
OFF_QI = OFF_VB + W_ATT
OFF_KA = OFF_QI + IDX_HEADS * IDX_DIM
OFF_VA = OFF_KA + HEAD_DIM
N_MAIN = OFF_VA + HEAD_DIM
N_SMALL = 4 * LANE


def _params(semantics, vmem_mib):
    return pltpu.CompilerParams(dimension_semantics=semantics,
                                vmem_limit_bytes=vmem_mib * MIB)


def _ada_kernel(c_ref, w_ref, b_ref, o_ref):
    c = c_ref[...]
    ca = (c * jax.nn.sigmoid(c)).astype(BF16)
    o_ref[...] = jnp.dot(ca, w_ref[...].astype(BF16), preferred_element_type=F32) + b_ref[...]


def _ada(c_pad, w, b):
    rows, d = c_pad.shape
    n = w.shape[1]
    bn = 512
    return pl.pallas_call(
        _ada_kernel,
        grid=(n // bn,),
        in_specs=[pl.BlockSpec((rows, d), lambda j: (0, 0)),
                  pl.BlockSpec((d, bn), lambda j: (0, j)),
                  pl.BlockSpec((1, bn), lambda j: (0, j))],
        out_specs=pl.BlockSpec((rows, bn), lambda j: (0, j)),
        out_shape=jax.ShapeDtypeStruct((rows, n), F32),
        compiler_params=_params(("arbitrary",), 40),
        name="ada_mod",
    )(c_pad, w, b)


def _norm_kernel(*refs, has_y, y_scale, modulated):
    it = iter(refs)
    x_ref = next(it)
    if has_y:
        y_ref = next(it)
        gate_ref = next(it)
    g_ref = next(it)
    if modulated:
        shift_ref = next(it)
        scale_ref = next(it)
    x = x_ref[...]
    if has_y:
        x = x + (y_scale * gate_ref[0]) * y_ref[...]
        xo_ref = next(it)
        if modulated:
            xo_ref[...] = x
    ms = jnp.mean(x * x, axis=-1, keepdims=True)
    nrm = x * lax.rsqrt(ms + RMS_EPS) * g_ref[...]
    if modulated:
        h_ref = next(it)
        h_ref[...] = (nrm * (1.0 + scale_ref[0]) + shift_ref[0]).astype(h_ref.dtype)
    else:
        xo_ref[...] = nrm


def _norm(x, g, *, batch, y=None, gate=None, y_scale=1.0, shift=None, scale=None):
    m, d = x.shape
    rows = 256
    per_b = m // batch // rows
    has_y = y is not None
    modulated = shift is not None
    row_spec = pl.BlockSpec((rows, d), lambda b, i: (b * per_b + i, 0))
    vec_spec = pl.BlockSpec((1, 1, d), lambda b, i: (b, 0, 0))
    ins, specs = [x], [row_spec]
    if has_y:
        ins += [y, gate]
        specs += [row_spec, vec_spec]
    ins.append(g)
    specs.append(pl.BlockSpec((1, d), lambda b, i: (0, 0)))
    if modulated:
        ins += [shift, scale]
        specs += [vec_spec, vec_spec]
    outs, out_specs = [], []
    if has_y:
        outs.append(jax.ShapeDtypeStruct((m, d), F32))
        out_specs.append(row_spec)
    if modulated:
        outs.append(jax.ShapeDtypeStruct((m, d), BF16))
        out_specs.append(row_spec)
    res = pl.pallas_call(
        functools.partial(_norm_kernel, has_y=has_y, y_scale=y_scale, modulated=modulated),
        grid=(batch, per_b),
        in_specs=specs,
        out_specs=out_specs,
        out_shape=outs,
        compiler_params=_params(("arbitrary", "arbitrary"), 48),
        name="resid_norm",
    )(*ins)
    return res


def _mm_kernel(a_ref, b_ref, o_ref):
    o_ref[...] = jnp.dot(a_ref[...], b_ref[...],
                         preferred_element_type=F32).astype(o_ref.dtype)


def _matmul(a, b, bm, bn, out_dtype, vmem_mib, name):
    m, k = a.shape
    n = b.shape[1]
    return pl.pallas_call(
        _mm_kernel,
        grid=(m // bm, n // bn),
        in_specs=[pl.BlockSpec((bm, k), lambda i, j: (i, 0)),
                  pl.BlockSpec((k, bn), lambda i, j: (0, j))],
        out_specs=pl.BlockSpec((bm, bn), lambda i, j: (i, j)),
        out_shape=jax.ShapeDtypeStruct((m, n), out_dtype),
        compiler_params=_params(("arbitrary", "arbitrary"), vmem_mib),
        name=name,
    )(a, b)


def _ffn_in_kernel(h_ref, wa_ref, wb_ref, o_ref):
    h = h_ref[...]
    a = jnp.dot(h, wa_ref[...], preferred_element_type=F32)
    b = jnp.dot(h, wb_ref[...], preferred_element_type=F32)
    o_ref[...] = (a * jax.nn.sigmoid(a) * b).astype(o_ref.dtype)


def _ffn_in(h, w_in):
    m, k = h.shape
    bm, bn = 1024, 256
    nb = D_FF // bn
    return pl.pallas_call(
        _ffn_in_kernel,
        grid=(m // bm, nb),
        in_specs=[pl.BlockSpec((bm, k), lambda i, j: (i, 0)),
                  pl.BlockSpec((k, bn), lambda i, j: (0, j)),
                  pl.BlockSpec((k, bn), lambda i, j: (0, j + nb))],
        out_specs=pl.BlockSpec((bm, bn), lambda i, j: (i, j)),
        out_shape=jax.ShapeDtypeStruct((m, D_FF), BF16),
        compiler_params=_params(("arbitrary", "arbitrary"), 40),
        name="ffn_in_swiglu",
    )(h, w_in, w_in)


def _merge_kernel(oa_ref, ob_ref, wa_ref, wb_ref, ga_ref, gb_ref, o_ref):
    ya = jnp.dot(oa_ref[...], wa_ref[...], preferred_element_type=F32)
    yb = jnp.dot(ob_ref[...], wb_ref[...], preferred_element_type=F32)
    ga = jax.nn.sigmoid(ga_ref[...].astype(F32))
    gb = jax.nn.sigmoid(gb_ref[...].astype(F32))
    o_ref[...] = (ga * ya + gb * yb).astype(o_ref.dtype)


def _merge(o_a, o_b, w_up_a, w_up_b, proj):
    m, k = o_a.shape
    bm, bn = 1024, 512
    gb_off = OFF_GATE_B // bn
    return pl.pallas_call(
        _merge_kernel,
        grid=(m // bm, D_MODEL // bn),
        in_specs=[pl.BlockSpec((bm, k), lambda i, j: (i, 0)),
                  pl.BlockSpec((bm, k), lambda i, j: (i, 0)),
                  pl.BlockSpec((k, bn), lambda i, j: (0, j)),
                  pl.BlockSpec((k, bn), lambda i, j: (0, j)),
                  pl.BlockSpec((bm, bn), lambda i, j: (i, j)),
                  pl.BlockSpec((bm, bn), lambda i, j: (i, j + gb_off))],
        out_specs=pl.BlockSpec((bm, bn), lambda i, j: (i, j)),
        out_shape=jax.ShapeDtypeStruct((m, D_MODEL), BF16),
        compiler_params=_params(("arbitrary", "arbitrary"), 40),
        name="gated_merge",
    )(o_a, o_b, w_up_a, w_up_b, proj, proj)


def _t5_bucket_np(dist):
    n = np.maximum(dist, 0)
    max_exact = N_BUCKETS // 2
    nf = np.maximum(n, max_exact).astype(np.float32)
    large = max_exact + (np.log(nf / max_exact) / math.log(MAX_DISTANCE / max_exact)
                         * (N_BUCKETS - max_exact)).astype(np.int32)
    large = np.minimum(large, N_BUCKETS - 1)
    return np.where(n < max_exact, n, large).astype(np.int32)


def _bias_tile_kernel(rb_ref, bk_ref, o_ref):
    h = pl.program_id(0)
    bk = bk_ref[...]
    far = rb_ref[N_BUCKETS - 1, h]
    acc = jnp.zeros(bk.shape, F32)
    for k in range(N_BUCKETS - 1):
        acc = jnp.where(bk == k, rb_ref[k, h] - far, acc)
    o_ref[0] = acc


def _bias_tiles(rel_bias, tq):
    r = np.arange(tq)[:, None]
    u = np.arange(2 * tq)[None, :]
    dist = tq + r - u
    bucket = np.where(dist >= 0, _t5_bucket_np(dist), N_BUCKETS - 1).astype(np.int32)
    return pl.pallas_call(
        _bias_tile_kernel,
        grid=(N_HEADS,),
        in_specs=[pl.BlockSpec(memory_space=pltpu.SMEM),
                  pl.BlockSpec((tq, 2 * tq), lambda h: (0, 0))],
        out_specs=pl.BlockSpec((1, tq, 2 * tq), lambda h: (h, 0, 0)),
        out_shape=jax.ShapeDtypeStruct((N_HEADS, tq, 2 * tq), F32),
        compiler_params=_params(("arbitrary",), 16),
        name="t5_bias_tiles",
    )(rel_bias, jnp.asarray(bucket))


def _forget_kernel(f_ref, b_ref, o_ref):
    z = f_ref[...] + b_ref[...]
    ls = jnp.minimum(z, 0.0) - jnp.log(1.0 + jnp.exp(-jnp.abs(z)))
    x = ls.T[0:N_HEADS, :]
    seq = x.shape[1]
    lane = lax.broadcasted_iota(jnp.int32, x.shape, 1)
    sh = 1
    while sh < seq:
        x = x + jnp.where(lane >= sh, pltpu.roll(x, sh, 1), 0.0)
        sh *= 2
    o_ref[0] = x


def _forget_cumsum(proj_small, b_forget_pad, batch, seq):
    return pl.pallas_call(
        _forget_kernel,
        grid=(batch,),
        in_specs=[pl.BlockSpec((seq, LANE), lambda b: (b, 3)),
                  pl.BlockSpec((1, LANE), lambda b: (0, 0))],
        out_specs=pl.BlockSpec((1, N_HEADS, seq), lambda b: (b, 0, 0)),
        out_shape=jax.ShapeDtypeStruct((batch, N_HEADS, seq), F32),
        compiler_params=_params(("arbitrary",), 32),
        name="forget_cumsum",
    )(proj_small, b_forget_pad)


_NT = (((1,), (1,)), ((), ()))


def _fox_kernel(q_ref, k_ref, v_ref, f_ref, o_ref, *, tq):
    h = pl.program_id(1)
    seq = q_ref.shape[0]
    frow = f_ref[0, pl.ds(h, 1), :]
    for i in range(seq // tq):
        s_len = (i + 1) * tq
        q = q_ref[i * tq:(i + 1) * tq, :]
        lg = lax.dot_general(q, k_ref[0:s_len, :], _NT, preferred_element_type=F32) * ATT_SCALE
        lg = lg - frow[:, 0:s_len]
        row = lax.broadcasted_iota(jnp.int32, lg.shape, 0) + i * tq
        col = lax.broadcasted_iota(jnp.int32, lg.shape, 1)
        lg = jnp.where(col <= row, lg, NEG_BIG)
        m = jnp.max(lg, axis=-1, keepdims=True)
        p = jnp.exp(lg - m)
        l = jnp.sum(p, axis=-1, keepdims=True)
        o = jnp.dot(p.astype(BF16), v_ref[0:s_len, :], preferred_element_type=F32)
        o_ref[i * tq:(i + 1) * tq, :] = (o / l).astype(o_ref.dtype)


def _fox_attention(proj, f_cum, batch, seq):
    tq = 256
    qc, kc, vc = OFF_QB // HEAD_DIM, OFF_KB // HEAD_DIM, OFF_VB // HEAD_DIM
    blk = (seq, HEAD_DIM)
    return pl.pallas_call(
        functools.partial(_fox_kernel, tq=tq),
        grid=(batch, N_HEADS),
        in_specs=[pl.BlockSpec(blk, lambda b, h: (b, qc + h)),
                  pl.BlockSpec(blk, lambda b, h: (b, kc + h)),
                  pl.BlockSpec(blk, lambda b, h: (b, vc + h)),
                  pl.BlockSpec((1, N_HEADS, seq), lambda b, h: (b, 0, 0))],
        out_specs=pl.BlockSpec(blk, lambda b, h: (b, h)),
        out_shape=jax.ShapeDtypeStruct((batch * seq, W_ATT), BF16),
        compiler_params=_params(("arbitrary", "arbitrary"), 32),
        name="fox_attention",
    )(proj, proj, proj, f_cum)


def _dsa_kernel(qi_ref, k0_ref, k1_ref, w_ref, qa_ref, ka_ref, va_ref, dc_ref, o_ref,
                key_ref, mb_ref, lg_ref, *, tq, topk):
    i = pl.program_id(1)
    seq = ka_ref.shape[0]
    shape = (tq, seq)

    k_even = k0_ref[...].astype(BF16)
    k_odd = k1_ref[...].astype(BF16)
    w = w_ref[...] * IDX_SCALE
    score = jnp.zeros(shape, F32)
    for hp in range(IDX_HEADS // 2):
        q2 = qi_ref[:, hp * LANE:(hp + 1) * LANE]
        for par, kk in ((0, k_even), (1, k_odd)):
            hh = 2 * hp + par
            s = lax.dot_general(q2, kk, _NT, preferred_element_type=F32)
            score = score + jnp.maximum(s, 0.0) * w[:, hh:hh + 1]

    t_pos = lax.broadcasted_iota(jnp.int32, shape, 0) + i * tq
    s_pos = lax.broadcasted_iota(jnp.int32, shape, 1)
    causal = s_pos <= t_pos
    score = jnp.where(causal, score, -jnp.inf)
    bits = lax.bitcast_convert_type(score, jnp.int32)
    key_ref[...] = jnp.where(bits < 0, bits ^ jnp.int32(0x7FFFFFFF), bits)

    def bit_step(j, thr):
        cand = thr + lax.shift_left(jnp.int32(1), 31 - j)
        cnt = jnp.sum(jnp.where(key_ref[...] >= cand, 1.0, 0.0), axis=-1, keepdims=True)
        return jnp.where(cnt >= topk, cand, thr)

    thr = lax.fori_loop(0, 32, bit_step, jnp.full((tq, 1), INT_MIN, jnp.int32))

    key = key_ref[...]
    ge = key >= thr
    mb_ref[...] = jnp.where(ge & causal, 0.0, NEG_BIG)
    n_ge = jnp.sum(jnp.where(ge, 1.0, 0.0), axis=-1, keepdims=True)
    tied = jnp.where((n_ge > topk) & (thr > KEY_NEG_INF), 1.0, 0.0)

    @pl.when(jnp.max(tied) > 0.0)
    def _():
        kk = key_ref[...]
        gt = kk > thr
        tie = jnp.where(kk == thr, 1.0, 0.0)
        room = topk - jnp.sum(jnp.where(gt, 1.0, 0.0), axis=-1, keepdims=True)
        x = tie
        sh = 1
        while sh < seq:
            x = x + jnp.where(s_pos >= sh, pltpu.roll(x, sh, 1), 0.0)
            sh *= 2
        take = (tie > 0.0) & (x - tie < room)
        mb_ref[...] = jnp.where((gt | take) & causal, 0.0, NEG_BIG)

    lg_ref[:, 0:tq] = jnp.zeros((tq, tq), F32)
    off = pl.multiple_of(i * tq, tq)
    ka = ka_ref[...]
    va = va_ref[...]
    for h in range(N_HEADS):
        q = qa_ref[:, h * HEAD_DIM:(h + 1) * HEAD_DIM]
        lg = lax.dot_general(q, ka, _NT, preferred_element_type=F32) * ATT_SCALE + mb_ref[...]
        lg_ref[:, tq:] = lg
        lg_ref[:, pl.ds(off, 2 * tq)] += dc_ref[h]
        lg = lg_ref[:, tq:]
        m = jnp.max(lg, axis=-1, keepdims=True)
        p = jnp.exp(lg - m)
        l = jnp.sum(p, axis=-1, keepdims=True)
        o = jnp.dot(p.astype(BF16), va, preferred_element_type=F32)
        o_ref[:, h * HEAD_DIM:(h + 1) * HEAD_DIM] = (o / l).astype(o_ref.dtype)


def _dsa_attention(proj, proj_small, bias_tiles, batch, seq):
    tq = LANE
    nq = seq // tq
    topk = min(TOPK_MAX, seq // 4)
    n_idx = IDX_HEADS * IDX_DIM
    return pl.pallas_call(
        functools.partial(_dsa_kernel, tq=tq, topk=topk),
        grid=(batch, nq),
        in_specs=[pl.BlockSpec((tq, n_idx), lambda b, i: (b * nq + i, OFF_QI // n_idx)),
                  pl.BlockSpec((seq, LANE), lambda b, i: (b, 0)),
                  pl.BlockSpec((seq, LANE), lambda b, i: (b, 1)),
                  pl.BlockSpec((tq, LANE), lambda b, i: (b * nq + i, 2)),
                  pl.BlockSpec((tq, W_ATT), lambda b, i: (b * nq + i, OFF_QA // W_ATT)),
                  pl.BlockSpec((seq, HEAD_DIM), lambda b, i: (b, OFF_KA // HEAD_DIM)),
                  pl.BlockSpec((seq, HEAD_DIM), lambda b, i: (b, OFF_VA // HEAD_DIM)),
                  pl.BlockSpec((N_HEADS, tq, 2 * tq), lambda b, i: (0, 0, 0))],
        out_specs=pl.BlockSpec((tq, W_ATT), lambda b, i: (b * nq + i, 0)),
        out_shape=jax.ShapeDtypeStruct((batch * seq, W_ATT), BF16),
        scratch_shapes=[pltpu.VMEM((tq, seq), jnp.int32),
                        pltpu.VMEM((tq, seq), F32),
                        pltpu.VMEM((tq, seq + tq), F32)],
        compiler_params=_params(("arbitrary", "arbitrary"), 40),
        name="dsa_attention",
    )(proj, proj_small, proj_small, proj_small, proj, proj, proj, bias_tiles)


def _mixer_weights(w_in):
    sizes = (W_ATT, HEAD_DIM, HEAD_DIM, IDX_HEADS * IDX_DIM, IDX_DIM, IDX_HEADS,
             W_ATT, W_ATT, W_ATT, N_HEADS, D_MODEL, D_MODEL)
    offs = np.concatenate([[0], np.cumsum(sizes)])
    seg = [w_in[:, int(offs[n]):int(offs[n + 1])] for n in range(len(sizes))]
    q_a, k_a, v_a, q_i, k_i, w_i, q_b, k_b, v_b, f_b, gate_a, gate_b = seg
    main = jnp.concatenate([gate_a, gate_b, q_a, q_b, k_b, v_b, q_i, k_a, v_a], axis=1)
    d = w_in.shape[0]
    z = lambda n: jnp.zeros((d, n), w_in.dtype)
    small = jnp.concatenate([k_i, z(LANE - IDX_DIM), z(LANE - IDX_DIM), k_i,
                             w_i, z(LANE - IDX_HEADS), f_b, z(LANE - N_HEADS)], axis=1)
    return main.astype(BF16), small.astype(BF16)


def _ffn(h, w_in, w_out):
    act = _ffn_in(h, w_in.astype(BF16))
    return _matmul(act, w_out.astype(BF16), 512, 512, F32, 56, "ffn_out")


def kernel(x, c, w_ada, b_ada, g_ffn1, ffn1_w_in, ffn1_w_out, g_mix, w_in, b_forget, rel_bias,
           w_up_a, w_up_b, w_o, g_ffn2, ffn2_w_in, ffn2_w_out, g_final):
    batch, seq, d = x.shape
    assert w_ada.shape[0] == 1, "single-layer trunk"
    xf = x.reshape(batch * seq, d)

    c_pad = jnp.zeros((16, d), F32).at[:batch].set(c)
    bias_tiles = _bias_tiles(rel_bias, LANE)
    for l in range(1):
        mod = _ada(c_pad, w_ada[l], b_ada[l][None, :])[:batch]
        sh1, sc1, gt1, sh2, sc2, gt2, sh3, sc3, gt3 = [
            v.reshape(batch, 1, d) for v in jnp.split(mod, N_MOD, axis=-1)]

        (h,) = _norm(xf, g_ffn1[l][None, :], batch=batch, shift=sh1, scale=sc1)
        y = _ffn(h, ffn1_w_in[l], ffn1_w_out[l])

        xf, h = _norm(xf, g_mix[l][None, :], batch=batch, y=y, gate=gt1, y_scale=0.5,
                      shift=sh2, scale=sc2)
        w_main, w_small = _mixer_weights(w_in[l])
        proj = _matmul(h, w_main, 1024, 768, BF16, 48, "mixer_proj")
        proj_small = _matmul(h, w_small, 1024, N_SMALL, F32, 40, "mixer_proj_small")
        b_f = jnp.zeros((1, LANE), F32).at[0, :N_HEADS].set(b_forget[l])
        f_cum = _forget_cumsum(proj_small, b_f, batch, seq)
        o_b = _fox_attention(proj, f_cum, batch, seq)
        o_a = _dsa_attention(proj, proj_small, bias_tiles, batch, seq)
        merged = _merge(o_a, o_b, w_up_a[l].astype(BF16), w_up_b[l].astype(BF16), proj)
        y = _matmul(merged, w_o[l].astype(BF16), 1024, 512, F32, 40, "mixer_out")

        xf, h = _norm(xf, g_ffn2[l][None, :], batch=batch, y=y, gate=gt2, y_scale=1.0,
                      shift=sh3, scale=sc3)
        y = _ffn(h, ffn2_w_in[l], ffn2_w_out[l])
    (out,) = _norm(xf, g_final[None, :], batch=batch, y=y, gate=gt3, y_scale=0.5)
    return out.reshape(batch, seq, d)
```

```python
import functools
import math

import numpy as np
import jax
import jax.numpy as jnp
from jax import lax
from jax.experimental import pallas as pl
from jax.experimental.pallas import tpu as pltpu

F32 = jnp.float32
BF16 = jnp.bfloat16

D_MODEL = 4096
HEAD_DIM = 128
N_HEADS = 16
W_ATT = N_HEADS * HEAD_DIM
IDX_HEADS = 16
IDX_DIM = 64
TOPK_MAX = 256
N_BUCKETS = 32
MAX_DISTANCE = 128
D_FF = 11008
N_MOD = 9
RMS_EPS = 1e-6
LOG2E = math.log2(math.e)
Q_SCALE = HEAD_DIM ** -0.5 * LOG2E
IDX_SCALE = IDX_HEADS ** -0.5 * IDX_DIM ** -0.5

LANE = 128
MIB = 1024 * 1024
NEG_BIG = -1e30
INT_MIN = -(2 ** 31)
KEY_NEG_INF = (0xFF800000 ^ 0x7FFFFFFF) - 2 ** 32

OFF_GATE_A = 0
OFF_GATE_B = OFF_GATE_A + D_MODEL
OFF_QA = OFF_GATE_B + D_MODEL
OFF_QB = OFF_QA + W_ATT
OFF_KB = OFF_QB + W_ATT
OFF_VB = OFF_KB + W_ATT
OFF_QI = OFF_VB + W_ATT
OFF_KA = OFF_QI + IDX_HEADS * IDX_DIM
OFF_VA = OFF_KA + HEAD_DIM
N_MAIN = OFF_VA + HEAD_DIM
N_SMALL = 4 * LANE

DSA_TQ = 256
DSA_GROUP = 512
DSA_LEAD = LANE


def _params(semantics, vmem_mib):
    return pltpu.CompilerParams(dimension_semantics=semantics,
                                vmem_limit_bytes=vmem_mib * MIB)


def _ada_kernel(c_ref, w_ref, b_ref, o_ref):
    c = c_ref[...]
    ca = (c * jax.nn.sigmoid(c)).astype(BF16)
    o_ref[...] = jnp.dot(ca, w_ref[...].astype(BF16), preferred_element_type=F32) + b_ref[...]


def _ada(c_pad, w, b):
    rows, d = c_pad.shape
    n = w.shape[1]
    bn = 512
    return pl.pallas_call(
        _ada_kernel,
        grid=(n // bn,),
        in_specs=[pl.BlockSpec((rows, d), lambda j: (0, 0)),
                  pl.BlockSpec((d, bn), lambda j: (0, j)),
                  pl.BlockSpec((1, bn), lambda j: (0, j))],
        out_specs=pl.BlockSpec((rows, bn), lambda j: (0, j)),
        out_shape=jax.ShapeDtypeStruct((rows, n), F32),
        compiler_params=_params(("arbitrary",), 40),
        name="ada_mod",
    )(c_pad, w, b)


def _norm_kernel(*refs, has_y, y_scale, modulated):
    it = iter(refs)
    x_ref = next(it)
    if has_y:
        y_ref = next(it)
        gate_ref = next(it)
    g_ref = next(it)
    if modulated:
        shift_ref = next(it)
        scale_ref = next(it)
    x = x_ref[...]
    if has_y:
        x = x + (y_scale * gate_ref[0]) * y_ref[...].astype(F32)
        xo_ref = next(it)
        if modulated:
            xo_ref[...] = x
    ms = jnp.mean(x * x, axis=-1, keepdims=True)
    nrm = x * lax.rsqrt(ms + RMS_EPS) * g_ref[...]
    if modulated:
        h_ref = next(it)
        h_ref[...] = (nrm * (1.0 + scale_ref[0]) + shift_ref[0]).astype(h_ref.dtype)
    else:
        xo_ref[...] = nrm


def _norm(x, g, *, batch, y=None, gate=None, y_scale=1.0, shift=None, scale=None):
    m, d = x.shape
    rows = 256
    per_b = m // batch // rows
    has_y = y is not None
    modulated = shift is not None
    row_spec = pl.BlockSpec((rows, d), lambda b, i: (b * per_b + i, 0))
    vec_spec = pl.BlockSpec((1, 1, d), lambda b, i: (b, 0, 0))
    ins, specs = [x], [row_spec]
    if has_y:
        ins += [y, gate]
        specs += [row_spec, vec_spec]
    ins.append(g)
    specs.append(pl.BlockSpec((1, d), lambda b, i: (0, 0)))
    if modulated:
        ins += [shift, scale]
        specs += [vec_spec, vec_spec]
    outs, out_specs = [], []
    if has_y:
        outs.append(jax.ShapeDtypeStruct((m, d), F32))
        out_specs.append(row_spec)
    if modulated:
        outs.append(jax.ShapeDtypeStruct((m, d), BF16))
        out_specs.append(row_spec)
    return pl.pallas_call(
        functools.partial(_norm_kernel, has_y=has_y, y_scale=y_scale, modulated=modulated),
        grid=(batch, per_b),
        in_specs=specs,
        out_specs=out_specs,
        out_shape=outs,
        compiler_params=_params(("arbitrary", "arbitrary"), 48),
        name="resid_norm",
    )(*ins)


def _mm_kernel(a_ref, b_ref, o_ref):
    o_ref[...] = jnp.dot(a_ref[...], b_ref[...],
                         preferred_element_type=F32).astype(o_ref.dtype)


def _matmul(a, b, bm, bn, out_dtype, vmem_mib, name):
    m, k = a.shape
    n = b.shape[1]
    return pl.pallas_call(
        _mm_kernel,
        grid=(m // bm, n // bn),
        in_specs=[pl.BlockSpec((bm, k), lambda i, j: (i, 0)),
                  pl.BlockSpec((k, bn), lambda i, j: (0, j))],
        out_specs=pl.BlockSpec((bm, bn), lambda i, j: (i, j)),
        out_shape=jax.ShapeDtypeStruct((m, n), out_dtype),
        compiler_params=_params(("arbitrary", "arbitrary"), vmem_mib),
        name=name,
    )(a, b)


def _mm_ws_kernel(a_ref, w_ref, o_ref, wbf_ref):
    @pl.when(pl.program_id(1) == 0)
    def _():
        wbf_ref[...] = w_ref[...].astype(BF16)

    o_ref[...] = jnp.dot(a_ref[...], wbf_ref[...],
                         preferred_element_type=F32).astype(o_ref.dtype)


def _matmul_ws(a, w, bm, bn, out_dtype, vmem_mib, name):
    m, k = a.shape
    n = w.shape[1]
    return pl.pallas_call(
        _mm_ws_kernel,
        grid=(n // bn, m // bm),
        in_specs=[pl.BlockSpec((bm, k), lambda j, i: (i, 0)),
                  pl.BlockSpec((k, bn), lambda j, i: (0, j))],
        out_specs=pl.BlockSpec((bm, bn), lambda j, i: (i, j)),
        out_shape=jax.ShapeDtypeStruct((m, n), out_dtype),
        scratch_shapes=[pltpu.VMEM((k, bn), BF16)],
        compiler_params=_params(("arbitrary", "arbitrary"), vmem_mib),
        name=name,
    )(a, w)


def _ffn_in_kernel(h_ref, wa_ref, wb_ref, o_ref, wa_bf, wb_bf):
    @pl.when(pl.program_id(1) == 0)
    def _():
        wa_bf[...] = wa_ref[...].astype(BF16)
        wb_bf[...] = wb_ref[...].astype(BF16)

    h = h_ref[...]
    a = jnp.dot(h, wa_bf[...], preferred_element_type=F32)
    b = jnp.dot(h, wb_bf[...], preferred_element_type=F32)
    o_ref[...] = (a * jax.nn.sigmoid(a) * b).astype(o_ref.dtype)


def _ffn_in(h, w_in):
    m, k = h.shape
    bm, bn = 1024, 256
    nb = D_FF // bn
    return pl.pallas_call(
        _ffn_in_kernel,
        grid=(nb, m // bm),
        in_specs=[pl.BlockSpec((bm, k), lambda j, i: (i, 0)),
                  pl.BlockSpec((k, bn), lambda j, i: (0, j)),
                  pl.BlockSpec((k, bn), lambda j, i: (0, j + nb))],
        out_specs=pl.BlockSpec((bm, bn), lambda j, i: (i, j)),
        out_shape=jax.ShapeDtypeStruct((m, D_FF), BF16),
        scratch_shapes=[pltpu.VMEM((k, bn), BF16), pltpu.VMEM((k, bn), BF16)],
        compiler_params=_params(("arbitrary", "arbitrary"), 48),
        name="ffn_in_swiglu",
    )(h, w_in, w_in)


def _merge_kernel(oa_ref, ob_ref, wa_ref, wb_ref, ga_ref, gb_ref, o_ref, wa_bf, wb_bf):
    @pl.when(pl.program_id(1) == 0)
    def _():
        wa_bf[...] = wa_ref[...].astype(BF16)
        wb_bf[...] = wb_ref[...].astype(BF16)

    ya = jnp.dot(oa_ref[...], wa_bf[...], preferred_element_type=F32)
    yb = jnp.dot(ob_ref[...], wb_bf[...], preferred_element_type=F32)
    ga = jax.nn.sigmoid(ga_ref[...].astype(F32))
    gb = jax.nn.sigmoid(gb_ref[...].astype(F32))
    o_ref[...] = (ga * ya + gb * yb).astype(o_ref.dtype)


def _merge(o_a, o_b, w_up_a, w_up_b, proj):
    m, k = o_a.shape
    bm, bn = 1024, 512
    gb_off = OFF_GATE_B // bn
    return pl.pallas_call(
        _merge_kernel,
        grid=(D_MODEL // bn, m // bm),
        in_specs=[pl.BlockSpec((bm, k), lambda j, i: (i, 0)),
                  pl.BlockSpec((bm, k), lambda j, i: (i, 0)),
                  pl.BlockSpec((k, bn), lambda j, i: (0, j)),
                  pl.BlockSpec((k, bn), lambda j, i: (0, j)),
                  pl.BlockSpec((bm, bn), lambda j, i: (i, j)),
                  pl.BlockSpec((bm, bn), lambda j, i: (i, j + gb_off))],
        out_specs=pl.BlockSpec((bm, bn), lambda j, i: (i, j)),
        out_shape=jax.ShapeDtypeStruct((m, D_MODEL), BF16),
        scratch_shapes=[pltpu.VMEM((k, bn), BF16), pltpu.VMEM((k, bn), BF16)],
        compiler_params=_params(("arbitrary", "arbitrary"), 48),
        name="gated_merge",
    )(o_a, o_b, w_up_a, w_up_b, proj, proj)


def _t5_bucket_np(dist):
    n = np.maximum(dist, 0)
    max_exact = N_BUCKETS // 2
    nf = np.maximum(n, max_exact).astype(np.float32)
    large = max_exact + (np.log(nf / max_exact) / math.log(MAX_DISTANCE / max_exact)
                         * (N_BUCKETS - max_exact)).astype(np.int32)
    large = np.minimum(large, N_BUCKETS - 1)
    return np.where(n < max_exact, n, large).astype(np.int32)


def _bias_tile_kernel(rb_ref, bk_ref, o_ref):
    h = pl.program_id(0)
    bk = bk_ref[...]
    far = rb_ref[N_BUCKETS - 1, h]
    acc = jnp.zeros(bk.shape, F32)
    for k in range(N_BUCKETS - 1):
        acc = jnp.where(bk == k, (rb_ref[k, h] - far) * LOG2E, acc)
    o_ref[0] = acc


def _bias_tiles(rel_bias, rows, lead):
    cols = lead + rows
    dist = lead + np.arange(rows)[:, None] - np.arange(cols)[None, :]
    bucket = np.where(dist >= 0, _t5_bucket_np(dist), N_BUCKETS - 1).astype(np.int32)
    assert _t5_bucket_np(np.array([lead + 1]))[0] == N_BUCKETS - 1
    return pl.pallas_call(
        _bias_tile_kernel,
        grid=(N_HEADS,),
        in_specs=[pl.BlockSpec(memory_space=pltpu.SMEM),
                  pl.BlockSpec((rows, cols), lambda h: (0, 0))],
        out_specs=pl.BlockSpec((1, rows, cols), lambda h: (h, 0, 0)),
        out_shape=jax.ShapeDtypeStruct((N_HEADS, rows, cols), F32),
        compiler_params=_params(("arbitrary",), 16),
        name="t5_bias_tiles",
    )(rel_bias, jnp.asarray(bucket))


def _forget_kernel(f_ref, b_ref, o_ref):
    z = f_ref[...] + b_ref[...]
    ls = jnp.minimum(z, 0.0) - jnp.log(1.0 + jnp.exp(-jnp.abs(z)))
    x = ls.T[0:N_HEADS, :]
    seq = x.shape[1]
    lane = lax.broadcasted_iota(jnp.int32, x.shape, 1)
    sh = 1
    while sh < seq:
        x = x + jnp.where(lane >= sh, pltpu.roll(x, sh, 1), 0.0)
        sh *= 2
    o_ref[0] = x * LOG2E


def _forget_cumsum(proj_small, b_forget_pad, batch, seq):
    return pl.pallas_call(
        _forget_kernel,
        grid=(batch,),
        in_specs=[pl.BlockSpec((seq, LANE), lambda b: (b, 3)),
                  pl.BlockSpec((1, LANE), lambda b: (0, 0))],
        out_specs=pl.BlockSpec((1, N_HEADS, seq), lambda b: (b, 0, 0)),
        out_shape=jax.ShapeDtypeStruct((batch, N_HEADS, seq), F32),
        compiler_params=_params(("arbitrary",), 32),
        name="forget_cumsum",
    )(proj_small, b_forget_pad)


_NT = (((1,), (1,)), ((), ()))


def _fox_kernel(q_ref, k_ref, v_ref, f_ref, o_ref, *, tq):
    h = pl.program_id(1)
    seq = q_ref.shape[0]
    frow = f_ref[0, pl.ds(h, 1), :]
    row = lax.broadcasted_iota(jnp.int32, (tq, tq), 0)
    col = lax.broadcasted_iota(jnp.int32, (tq, tq), 1)
    diag_mask = jnp.where(col <= row, 0.0, NEG_BIG)
    for i in range(seq // tq):
        lo, hi = i * tq, (i + 1) * tq
        q = q_ref[lo:hi, :]
        lg_d = (lax.dot_general(q, k_ref[lo:hi, :], _NT, preferred_element_type=F32)
                - frow[:, lo:hi] + diag_mask)
        m = jnp.max(lg_d, axis=-1, keepdims=True)
        if i > 0:
            lg_p = (lax.dot_general(q, k_ref[0:lo, :], _NT, preferred_element_type=F32)
                    - frow[:, 0:lo])
            m = jnp.maximum(m, jnp.max(lg_p, axis=-1, keepdims=True))
        p_d = jnp.exp2(lg_d - m)
        l = jnp.sum(p_d, axis=-1, keepdims=True)
        o = jnp.dot(p_d.astype(BF16), v_ref[lo:hi, :], preferred_element_type=F32)
        if i > 0:
            p_p = jnp.exp2(lg_p - m)
            l = l + jnp.sum(p_p, axis=-1, keepdims=True)
            o = o + jnp.dot(p_p.astype(BF16), v_ref[0:lo, :], preferred_element_type=F32)
        o_ref[lo:hi, :] = (o / l).astype(o_ref.dtype)


def _fox_attention(proj, f_cum, batch, seq):
    tq = 256
    qc, kc, vc = OFF_QB // HEAD_DIM, OFF_KB // HEAD_DIM, OFF_VB // HEAD_DIM
    blk = (seq, HEAD_DIM)
    return pl.pallas_call(
        functools.partial(_fox_kernel, tq=tq),
        grid=(batch, N_HEADS),
        in_specs=[pl.BlockSpec(blk, lambda b, h: (b, qc + h)),
                  pl.BlockSpec(blk, lambda b, h: (b, kc + h)),
                  pl.BlockSpec(blk, lambda b, h: (b, vc + h)),
                  pl.BlockSpec((1, N_HEADS, seq), lambda b, h: (b, 0, 0))],
        out_specs=pl.BlockSpec(blk, lambda b, h: (b, h)),
        out_shape=jax.ShapeDtypeStruct((batch * seq, W_ATT), BF16),
        compiler_params=_params(("arbitrary", "arbitrary"), 32),
        name="fox_attention",
    )(proj, proj, proj, f_cum)


def _dsa_kernel(qi_ref, k0_ref, k1_ref, w_ref, qa_ref, ka_ref, va_ref, dc_ref, o_ref,
                key_ref, mb_ref, lg_ref, *, t_group, sk, topk):
    tq = qa_ref.shape[0]
    t0 = t_group + pl.program_id(1) * tq
    shape = (tq, sk)

    w_lane = lax.broadcasted_iota(jnp.int32, (tq, LANE), 1)
    mb_ref[...] = jnp.zeros(shape, F32)

    def idx_pair(hp, carry):
        q2 = qi_ref[:, pl.ds(pl.multiple_of(hp * LANE, LANE), LANE)]
        w = w_ref[...]
        acc = mb_ref[...]
        for par, k_ref in ((0, k0_ref), (1, k1_ref)):
            s = lax.dot_general(q2, k_ref[0:sk, :].astype(BF16), _NT, preferred_element_type=F32)
            w_h = jnp.sum(jnp.where(w_lane == 2 * hp + par, w, 0.0), axis=-1, keepdims=True)
            acc = acc + jnp.maximum(s, 0.0) * w_h
        mb_ref[...] = acc
        return carry

    lax.fori_loop(0, IDX_HEADS // 2, idx_pair, 0)

    t_pos = lax.broadcasted_iota(jnp.int32, shape, 0) + t0
    s_pos = lax.broadcasted_iota(jnp.int32, shape, 1)
    causal = s_pos <= t_pos
    score = jnp.where(causal, mb_ref[...], -jnp.inf)
    bits = lax.bitcast_convert_type(score, jnp.int32)
    key_ref[...] = jnp.where(bits < 0, bits ^ jnp.int32(0x7FFFFFFF), bits)

    def bit_step(j, thr):
        cand = thr + lax.shift_left(jnp.int32(1), 31 - j)
        cnt = jnp.sum(jnp.where(key_ref[...] >= cand, 1.0, 0.0), axis=-1, keepdims=True)
        return jnp.where(cnt >= topk, cand, thr)

    thr = lax.fori_loop(0, 32, bit_step, jnp.full((tq, 1), INT_MIN, jnp.int32))

    key = key_ref[...]
    ge = key >= thr
    mb_ref[...] = jnp.where(ge & causal, 0.0, NEG_BIG)
    n_ge = jnp.sum(jnp.where(ge, 1.0, 0.0), axis=-1, keepdims=True)
    tied = jnp.where((n_ge > topk) & (thr > KEY_NEG_INF), 1.0, 0.0)

    @pl.when(jnp.max(tied) > 0.0)
    def _():
        kk = key_ref[...]
        gt = kk > thr
        tie = jnp.where(kk == thr, 1.0, 0.0)
        room = topk - jnp.sum(jnp.where(gt, 1.0, 0.0), axis=-1, keepdims=True)
        x = tie
        sh = 1
        while sh < sk:
            x = x + jnp.where(s_pos >= sh, pltpu.roll(x, sh, 1), 0.0)
            sh *= 2
        take = (tie > 0.0) & (x - tie < room)
        mb_ref[...] = jnp.where((gt | take) & causal, 0.0, NEG_BIG)

    lg_ref[:, 0:DSA_LEAD] = jnp.zeros((tq, DSA_LEAD), F32)
    off = pl.multiple_of(t0, LANE)

    def head(h, carry):
        hcol = pl.ds(pl.multiple_of(h * HEAD_DIM, HEAD_DIM), HEAD_DIM)
        q = qa_ref[:, hcol]
        lg_ref[:, DSA_LEAD:] = (lax.dot_general(q, ka_ref[0:sk, :], _NT, preferred_element_type=F32)
                                + mb_ref[...])
        lg_ref[:, pl.ds(off, DSA_LEAD + tq)] += dc_ref[h]
        lg = lg_ref[:, DSA_LEAD:]
        m = jnp.max(lg, axis=-1, keepdims=True)
        p = jnp.exp2(lg - m)
        l = jnp.sum(p, axis=-1, keepdims=True)
        o = jnp.dot(p.astype(BF16), va_ref[0:sk, :], preferred_element_type=F32)
        o_ref[0, :, hcol] = (o / l).astype(o_ref.dtype)
        return carry

    lax.fori_loop(0, N_HEADS, head, 0)


def _dsa_attention(proj, proj_small, bias_tiles, batch, seq):
    tq, grp = DSA_TQ, DSA_GROUP
    per_grp = grp // tq
    nq = seq // tq
    topk = min(TOPK_MAX, seq // 4)
    n_idx = IDX_HEADS * IDX_DIM
    outs = []
    for g in range(seq // grp):
        sk = (g + 1) * grp
        qrow = lambda b, j, g=g: b * nq + g * per_grp + j
        out = pl.pallas_call(
            functools.partial(_dsa_kernel, t_group=g * grp, sk=sk, topk=topk),
            grid=(batch, per_grp),
            in_specs=[pl.BlockSpec((tq, n_idx), lambda b, j, q=qrow: (q(b, j), OFF_QI // n_idx)),
                      pl.BlockSpec((seq, LANE), lambda b, j: (b, 0)),
                      pl.BlockSpec((seq, LANE), lambda b, j: (b, 1)),
                      pl.BlockSpec((tq, LANE), lambda b, j, q=qrow: (q(b, j), 2)),
                      pl.BlockSpec((tq, W_ATT), lambda b, j, q=qrow: (q(b, j), OFF_QA // W_ATT)),
                      pl.BlockSpec((seq, HEAD_DIM), lambda b, j: (b, OFF_KA // HEAD_DIM)),
                      pl.BlockSpec((seq, HEAD_DIM), lambda b, j: (b, OFF_VA // HEAD_DIM)),
                      pl.BlockSpec((N_HEADS, tq, DSA_LEAD + tq), lambda b, j: (0, 0, 0),
                                   pipeline_mode=pl.Buffered(1))],
            out_specs=pl.BlockSpec((1, tq, W_ATT), lambda b, j: (b, j, 0)),
            out_shape=jax.ShapeDtypeStruct((batch, grp, W_ATT), BF16),
            scratch_shapes=[pltpu.VMEM((tq, sk), jnp.int32),
                            pltpu.VMEM((tq, sk), F32),
                            pltpu.VMEM((tq, DSA_LEAD + sk), F32)],
            compiler_params=_params(("arbitrary", "arbitrary"), 48),
            name=f"dsa_attention_k{sk}",
        )(proj, proj_small, proj_small, proj_small, proj, proj, proj, bias_tiles)
        outs.append(out)
    return jnp.concatenate(outs, axis=1).reshape(batch * seq, W_ATT)


def _mixer_weights(w_in):
    sizes = (W_ATT, HEAD_DIM, HEAD_DIM, IDX_HEADS * IDX_DIM, IDX_DIM, IDX_HEADS,
             W_ATT, W_ATT, W_ATT, N_HEADS, D_MODEL, D_MODEL)
    offs = np.concatenate([[0], np.cumsum(sizes)])
    seg = [w_in[:, int(offs[n]):int(offs[n + 1])] for n in range(len(sizes))]
    q_a, k_a, v_a, q_i, k_i, w_i, q_b, k_b, v_b, f_b, gate_a, gate_b = seg
    c = lambda a: a.astype(BF16)
    main = jnp.concatenate([c(gate_a), c(gate_b), c(q_a * Q_SCALE), c(q_b * Q_SCALE), c(k_b),
                            c(v_b), c(q_i), c(k_a), c(v_a)], axis=1)
    d = w_in.shape[0]
    z = lambda n: jnp.zeros((d, n), BF16)
    small = jnp.concatenate([c(k_i), z(LANE - IDX_DIM), z(LANE - IDX_DIM), c(k_i),
                             c(w_i * IDX_SCALE), z(LANE - IDX_HEADS),
                             c(f_b), z(LANE - N_HEADS)], axis=1)
    return main, small


def _ffn(h, w_in, w_out):
    act = _ffn_in(h, w_in)
    return _matmul(act, w_out.astype(BF16), 512, 512, BF16, 56, "ffn_out")


def kernel(x, c, w_ada, b_ada, g_ffn1, ffn1_w_in, ffn1_w_out, g_mix, w_in, b_forget, rel_bias,
           w_up_a, w_up_b, w_o, g_ffn2, ffn2_w_in, ffn2_w_out, g_final):
    batch, seq, d = x.shape
    assert w_ada.shape[0] == 1, "single-layer trunk"
    xf = x.reshape(batch * seq, d)

    c_pad = jnp.zeros((16, d), F32).at[:batch].set(c)
    bias_tiles = _bias_tiles(rel_bias, DSA_TQ, DSA_LEAD)
    mod = _ada(c_pad, w_ada[0], b_ada[0][None, :])[:batch]
    sh1, sc1, gt1, sh2, sc2, gt2, sh3, sc3, gt3 = [
        v.reshape(batch, 1, d) for v in jnp.split(mod, N_MOD, axis=-1)]

    (h,) = _norm(xf, g_ffn1[0][None, :], batch=batch, shift=sh1, scale=sc1)
    y = _ffn(h, ffn1_w_in[0], ffn1_w_out[0])

    xf, h = _norm(xf, g_mix[0][None, :], batch=batch, y=y, gate=gt1, y_scale=0.5,
                  shift=sh2, scale=sc2)
    w_main, w_small = _mixer_weights(w_in[0])
    proj = _matmul(h, w_main, 1024, 768, BF16, 48, "mixer_proj")
    proj_small = _matmul(h, w_small, 1024, N_SMALL, F32, 40, "mixer_proj_small")
    b_f = jnp.zeros((1, LANE), F32).at[0, :N_HEADS].set(b_forget[0])
    f_cum = _forget_cumsum(proj_small, b_f, batch, seq)
    o_b = _fox_attention(proj, f_cum, batch, seq)
    o_a = _dsa_attention(proj, proj_small, bias_tiles, batch, seq)
    merged = _merge(o_a, o_b, w_up_a[0], w_up_b[0], proj)
    y = _matmul_ws(merged, w_o[0], 1024, 512, BF16, 48, "mixer_out")

    xf, h = _norm(xf, g_ffn2[0][None, :], batch=batch, y=y, gate=gt2, y_scale=1.0,
                  shift=sh3, scale=sc3)
    y = _ffn(h, ffn2_w_in[0], ffn2_w_out[0])
    (out,) = _norm(xf, g_final[None, :], batch=batch, y=y, gate=gt3, y_scale=0.5)
    return out.reshape(batch, seq, d)
```

```python
import functools
import math

import numpy as np
import jax
import jax.numpy as jnp
from jax import lax
from jax.experimental import pallas as pl
from jax.experimental.pallas import tpu as pltpu

F32 = jnp.float32
BF16 = jnp.bfloat16

D_MODEL = 4096
HEAD_DIM = 128
N_HEADS = 16
W_ATT = N_HEADS * HEAD_DIM
IDX_HEADS = 16
IDX_DIM = 64
TOPK_MAX = 256
N_BUCKETS = 32
MAX_DISTANCE = 128
D_FF = 11008
N_MOD = 9
RMS_EPS = 1e-6
LOG2E = math.log2(math.e)
Q_SCALE = HEAD_DIM ** -0.5 * LOG2E
IDX_SCALE = IDX_HEADS ** -0.5 * IDX_DIM ** -0.5

LANE = 128
MIB = 1024 * 1024
NEG_BIG = -1e30
INT_MIN = -(2 ** 31)
KEY_NEG_INF = (0xFF800000 ^ 0x7FFFFFFF) - 2 ** 32

OFF_GATE_A = 0
OFF_GATE_B = OFF_GATE_A + D_MODEL
OFF_QA = OFF_GATE_B + D_MODEL
OFF_QB = OFF_QA + W_ATT
OFF_KB = OFF_QB + W_ATT
OFF_VB = OFF_KB + W_ATT
OFF_QI = OFF_VB + W_ATT
OFF_KA = OFF_QI + IDX_HEADS * IDX_DIM
OFF_VA = OFF_KA + HEAD_DIM
PROJ_BN = 512
N_MAIN = OFF_KA + PROJ_BN
N_SMALL = 4 * LANE

DSA_TQ = 256
DSA_GROUP = 512
DSA_LEAD = LANE


def _params(semantics, vmem_mib):
    return pltpu.CompilerParams(dimension_semantics=semantics,
                                vmem_limit_bytes=vmem_mib * MIB)


def _ada_kernel(c_ref, w_ref, b_ref, o_ref):
    c = c_ref[...]
    ca = (c * jax.nn.sigmoid(c)).astype(BF16)
    o_ref[...] = jnp.dot(ca, w_ref[...].astype(BF16), preferred_element_type=F32) + b_ref[...]


def _ada(c_pad, w, b):
    rows, d = c_pad.shape
    n = w.shape[1]
    bn = 512
    return pl.pallas_call(
        _ada_kernel,
        grid=(n // bn,),
        in_specs=[pl.BlockSpec((rows, d), lambda j: (0, 0)),
                  pl.BlockSpec((d, bn), lambda j: (0, j)),
                  pl.BlockSpec((1, bn), lambda j: (0, j))],
        out_specs=pl.BlockSpec((rows, bn), lambda j: (0, j)),
        out_shape=jax.ShapeDtypeStruct((rows, n), F32),
        compiler_params=_params(("arbitrary",), 40),
        name="ada_mod",
    )(c_pad, w, b)


def _norm_kernel(*refs, has_y, y_scale, modulated):
    it = iter(refs)
    x_ref = next(it)
    if has_y:
        y_ref = next(it)
        gate_ref = next(it)
    g_ref = next(it)
    if modulated:
        shift_ref = next(it)
        scale_ref = next(it)
    x = x_ref[...]
    if has_y:
        x = x + (y_scale * gate_ref[0]) * y_ref[...].astype(F32)
        xo_ref = next(it)
        if modulated:
            xo_ref[...] = x
    ms = jnp.mean(x * x, axis=-1, keepdims=True)
    nrm = x * lax.rsqrt(ms + RMS_EPS) * g_ref[...]
    if modulated:
        h_ref = next(it)
        h_ref[...] = (nrm * (1.0 + scale_ref[0]) + shift_ref[0]).astype(h_ref.dtype)
    else:
        xo_ref[...] = nrm


def _norm(x, g, *, batch, y=None, gate=None, y_scale=1.0, shift=None, scale=None):
    m, d = x.shape
    rows = 256
    per_b = m // batch // rows
    has_y = y is not None
    modulated = shift is not None
    row_spec = pl.BlockSpec((rows, d), lambda b, i: (b * per_b + i, 0))
    vec_spec = pl.BlockSpec((1, 1, d), lambda b, i: (b, 0, 0))
    ins, specs = [x], [row_spec]
    if has_y:
        ins += [y, gate]
        specs += [row_spec, vec_spec]
    ins.append(g)
    specs.append(pl.BlockSpec((1, d), lambda b, i: (0, 0)))
    if modulated:
        ins += [shift, scale]
        specs += [vec_spec, vec_spec]
    outs, out_specs = [], []
    if has_y:
        outs.append(jax.ShapeDtypeStruct((m, d), F32))
        out_specs.append(row_spec)
    if modulated:
        outs.append(jax.ShapeDtypeStruct((m, d), BF16))
        out_specs.append(row_spec)
    return pl.pallas_call(
        functools.partial(_norm_kernel, has_y=has_y, y_scale=y_scale, modulated=modulated),
        grid=(batch, per_b),
        in_specs=specs,
        out_specs=out_specs,
        out_shape=outs,
        compiler_params=_params(("arbitrary", "arbitrary"), 48),
        name="resid_norm",
    )(*ins)


def _mm_kernel(a_ref, b_ref, o_ref):
    o_ref[...] = jnp.dot(a_ref[...], b_ref[...],
                         preferred_element_type=F32).astype(o_ref.dtype)


def _matmul(a, b, bm, bn, out_dtype, vmem_mib, name):
    m, k = a.shape
    n = b.shape[1]
    return pl.pallas_call(
        _mm_kernel,
        grid=(m // bm, n // bn),
        in_specs=[pl.BlockSpec((bm, k), lambda i, j: (i, 0)),
                  pl.BlockSpec((k, bn), lambda i, j: (0, j))],
        out_specs=pl.BlockSpec((bm, bn), lambda i, j: (i, j)),
        out_shape=jax.ShapeDtypeStruct((m, n), out_dtype),
        compiler_params=_params(("arbitrary", "arbitrary"), vmem_mib),
        name=name,
    )(a, b)


def _with_stationary_weights(compute, w_refs, wbf_refs):
    first = pl.program_id(1) == 0

    @pl.when(first)
    def _():
        ws = [w[...].astype(BF16) for w in w_refs]
        for s, w in zip(wbf_refs, ws):
            s[...] = w
        compute(ws)

    @pl.when(jnp.logical_not(first))
    def _():
        compute([s[...] for s in wbf_refs])


def _mm_ws_kernel(a_ref, w_ref, o_ref, wbf_ref):
    def compute(ws):
        o_ref[...] = jnp.dot(a_ref[...], ws[0], preferred_element_type=F32).astype(o_ref.dtype)

    _with_stationary_weights(compute, [w_ref], [wbf_ref])


def _matmul_ws(a, w, bm, bn, out_dtype, vmem_mib, name):
    m, k = a.shape
    n = w.shape[1]
    return pl.pallas_call(
        _mm_ws_kernel,
        grid=(n // bn, m // bm),
        in_specs=[pl.BlockSpec((bm, k), lambda j, i: (i, 0)),
                  pl.BlockSpec((k, bn), lambda j, i: (0, j))],
        out_specs=pl.BlockSpec((bm, bn), lambda j, i: (i, j)),
        out_shape=jax.ShapeDtypeStruct((m, n), out_dtype),
        scratch_shapes=[pltpu.VMEM((k, bn), BF16)],
        compiler_params=_params(("arbitrary", "arbitrary"), vmem_mib),
        name=name,
    )(a, w)


def _ffn_in_kernel(h_ref, wa_ref, wb_ref, wo_ref, o_ref, wo_bf_ref, wa_bf, wb_bf):
    def compute(ws):
        h = h_ref[...]
        a = jnp.dot(h, ws[0], preferred_element_type=F32)
        b = jnp.dot(h, ws[1], preferred_element_type=F32)
        o_ref[...] = (a * jax.nn.sigmoid(a) * b).astype(o_ref.dtype)

    _with_stationary_weights(compute, [wa_ref, wb_ref], [wa_bf, wb_bf])
    wo_bf_ref[...] = wo_ref[...].astype(BF16)


def _ffn_in(h, w_in, w_out):
    m, k = h.shape
    bm, bn = 1024, 256
    nb, mb = D_FF // bn, m // bm
    wo_rows = D_FF // (nb * mb)
    assert wo_rows * nb * mb == D_FF and w_out.shape[0] == D_FF
    wo_spec = pl.BlockSpec((wo_rows, w_out.shape[1]), lambda j, i: (j * mb + i, 0))
    return pl.pallas_call(
        _ffn_in_kernel,
        grid=(nb, mb),
        in_specs=[pl.BlockSpec((bm, k), lambda j, i: (i, 0)),
                  pl.BlockSpec((k, bn), lambda j, i: (0, j)),
                  pl.BlockSpec((k, bn), lambda j, i: (0, j + nb)),
                  wo_spec],
        out_specs=[pl.BlockSpec((bm, bn), lambda j, i: (i, j)), wo_spec],
        out_shape=[jax.ShapeDtypeStruct((m, D_FF), BF16),
                   jax.ShapeDtypeStruct(w_out.shape, BF16)],
        scratch_shapes=[pltpu.VMEM((k, bn), BF16), pltpu.VMEM((k, bn), BF16)],
        compiler_params=_params(("arbitrary", "arbitrary"), 48),
        name="ffn_in_swiglu",
    )(h, w_in, w_in, w_out)


def _merge_kernel(oa_ref, ob_ref, wa_ref, wb_ref, ga_ref, gb_ref, o_ref, wa_bf, wb_bf):
    def compute(ws):
        ya = jnp.dot(oa_ref[...], ws[0], preferred_element_type=F32)
        yb = jnp.dot(ob_ref[...], ws[1], preferred_element_type=F32)
        ga = jax.nn.sigmoid(ga_ref[...].astype(F32))
        gb = jax.nn.sigmoid(gb_ref[...].astype(F32))
        o_ref[...] = (ga * ya + gb * yb).astype(o_ref.dtype)

    _with_stationary_weights(compute, [wa_ref, wb_ref], [wa_bf, wb_bf])


def _merge(o_a, o_b, w_up_a, w_up_b, proj):
    m, k = o_a.shape
    bm, bn = 1024, 512
    gb_off = OFF_GATE_B // bn
    return pl.pallas_call(
        _merge_kernel,
        grid=(D_MODEL // bn, m // bm),
        in_specs=[pl.BlockSpec((bm, k), lambda j, i: (i, 0)),
                  pl.BlockSpec((bm, k), lambda j, i: (i, 0)),
                  pl.BlockSpec((k, bn), lambda j, i: (0, j)),
                  pl.BlockSpec((k, bn), lambda j, i: (0, j)),
                  pl.BlockSpec((bm, bn), lambda j, i: (i, j)),
                  pl.BlockSpec((bm, bn), lambda j, i: (i, j + gb_off))],
        out_specs=pl.BlockSpec((bm, bn), lambda j, i: (i, j)),
        out_shape=jax.ShapeDtypeStruct((m, D_MODEL), BF16),
        scratch_shapes=[pltpu.VMEM((k, bn), BF16), pltpu.VMEM((k, bn), BF16)],
        compiler_params=_params(("arbitrary", "arbitrary"), 48),
        name="gated_merge",
    )(o_a, o_b, w_up_a, w_up_b, proj, proj)


def _t5_bucket_np(dist):
    n = np.maximum(dist, 0)
    max_exact = N_BUCKETS // 2
    nf = np.maximum(n, max_exact).astype(np.float32)
    large = max_exact + (np.log(nf / max_exact) / math.log(MAX_DISTANCE / max_exact)
                         * (N_BUCKETS - max_exact)).astype(np.int32)
    large = np.minimum(large, N_BUCKETS - 1)
    return np.where(n < max_exact, n, large).astype(np.int32)


def _bias_tile_kernel(rb_ref, bk_ref, o_ref):
    h = pl.program_id(0)
    bk = bk_ref[...]
    far = rb_ref[N_BUCKETS - 1, h]
    acc = jnp.zeros(bk.shape, F32)
    for k in range(N_BUCKETS - 1):
        acc = jnp.where(bk == k, (rb_ref[k, h] - far) * LOG2E, acc)
    o_ref[0] = acc


def _bias_tiles(rel_bias, rows, lead):
    cols = lead + rows
    dist = lead + np.arange(rows)[:, None] - np.arange(cols)[None, :]
    bucket = np.where(dist >= 0, _t5_bucket_np(dist), N_BUCKETS - 1).astype(np.int32)
    assert _t5_bucket_np(np.array([lead + 1]))[0] == N_BUCKETS - 1
    return pl.pallas_call(
        _bias_tile_kernel,
        grid=(N_HEADS,),
        in_specs=[pl.BlockSpec(memory_space=pltpu.SMEM),
                  pl.BlockSpec((rows, cols), lambda h: (0, 0))],
        out_specs=pl.BlockSpec((1, rows, cols), lambda h: (h, 0, 0)),
        out_shape=jax.ShapeDtypeStruct((N_HEADS, rows, cols), F32),
        compiler_params=_params(("arbitrary",), 16),
        name="t5_bias_tiles",
    )(rel_bias, jnp.asarray(bucket))


def _forget_kernel(f_ref, b_ref, o_ref):
    z = f_ref[...] + b_ref[...]
    ls = jnp.minimum(z, 0.0) - jnp.log(1.0 + jnp.exp(-jnp.abs(z)))
    x = ls.T[0:N_HEADS, :]
    seq = x.shape[1]
    lane = lax.broadcasted_iota(jnp.int32, x.shape, 1)
    sh = 1
    while sh < seq:
        x = x + jnp.where(lane >= sh, pltpu.roll(x, sh, 1), 0.0)
        sh *= 2
    o_ref[0] = x * LOG2E


def _forget_cumsum(proj_small, b_forget_pad, batch, seq):
    return pl.pallas_call(
        _forget_kernel,
        grid=(batch,),
        in_specs=[pl.BlockSpec((seq, LANE), lambda b: (b, 3)),
                  pl.BlockSpec((1, LANE), lambda b: (0, 0))],
        out_specs=pl.BlockSpec((1, N_HEADS, seq), lambda b: (b, 0, 0)),
        out_shape=jax.ShapeDtypeStruct((batch, N_HEADS, seq), F32),
        compiler_params=_params(("arbitrary",), 32),
        name="forget_cumsum",
    )(proj_small, b_forget_pad)


_NT = (((1,), (1,)), ((), ()))


def _fox_kernel(q_ref, k_ref, v_ref, f_ref, o_ref, *, tq):
    h = pl.program_id(1)
    seq = q_ref.shape[0]
    frow = f_ref[0, pl.ds(h, 1), :]
    row = lax.broadcasted_iota(jnp.int32, (tq, tq), 0)
    col = lax.broadcasted_iota(jnp.int32, (tq, tq), 1)
    diag_mask = jnp.where(col <= row, 0.0, NEG_BIG)
    for i in range(seq // tq):
        lo, hi = i * tq, (i + 1) * tq
        q = q_ref[lo:hi, :]
        lg_d = (lax.dot_general(q, k_ref[lo:hi, :], _NT, preferred_element_type=F32)
                - frow[:, lo:hi] + diag_mask)
        m = jnp.max(lg_d, axis=-1, keepdims=True)
        if i > 0:
            lg_p = (lax.dot_general(q, k_ref[0:lo, :], _NT, preferred_element_type=F32)
                    - frow[:, 0:lo])
            m = jnp.maximum(m, jnp.max(lg_p, axis=-1, keepdims=True))
        p_d = jnp.exp2(lg_d - m)
        l = jnp.sum(p_d, axis=-1, keepdims=True)
        o = jnp.dot(p_d.astype(BF16), v_ref[lo:hi, :], preferred_element_type=F32)
        if i > 0:
            p_p = jnp.exp2(lg_p - m)
            l = l + jnp.sum(p_p, axis=-1, keepdims=True)
            o = o + jnp.dot(p_p.astype(BF16), v_ref[0:lo, :], preferred_element_type=F32)
        o_ref[lo:hi, :] = (o / l).astype(o_ref.dtype)


def _fox_attention(proj, f_cum, batch, seq):
    tq = 256
    qc, kc, vc = OFF_QB // HEAD_DIM, OFF_KB // HEAD_DIM, OFF_VB // HEAD_DIM
    blk = (seq, HEAD_DIM)
    return pl.pallas_call(
        functools.partial(_fox_kernel, tq=tq),
        grid=(batch, N_HEADS),
        in_specs=[pl.BlockSpec(blk, lambda b, h: (b, qc + h)),
                  pl.BlockSpec(blk, lambda b, h: (b, kc + h)),
                  pl.BlockSpec(blk, lambda b, h: (b, vc + h)),
                  pl.BlockSpec((1, N_HEADS, seq), lambda b, h: (b, 0, 0))],
        out_specs=pl.BlockSpec(blk, lambda b, h: (b, h)),
        out_shape=jax.ShapeDtypeStruct((batch * seq, W_ATT), BF16),
        compiler_params=_params(("arbitrary", "arbitrary"), 32),
        name="fox_attention",
    )(proj, proj, proj, f_cum)


def _dsa_kernel(qi_ref, k0_ref, k1_ref, w_ref, qa_ref, ka_ref, va_ref, dc_ref, o_ref,
                key_ref, mb_ref, lg_ref, *, t_group, sk, topk):
    tq = qa_ref.shape[0]
    t0 = t_group + pl.program_id(1) * tq
    shape = (tq, sk)

    w_lane = lax.broadcasted_iota(jnp.int32, (tq, LANE), 1)
    mb_ref[...] = jnp.zeros(shape, F32)

    def idx_pair(hp, carry):
        q2 = qi_ref[:, pl.ds(pl.multiple_of(hp * LANE, LANE), LANE)]
        w = w_ref[...]
        acc = mb_ref[...]
        for par, k_ref in ((0, k0_ref), (1, k1_ref)):
            s = lax.dot_general(q2, k_ref[0:sk, :].astype(BF16), _NT, preferred_element_type=F32)
            w_h = jnp.sum(jnp.where(w_lane == 2 * hp + par, w, 0.0), axis=-1, keepdims=True)
            acc = acc + jnp.maximum(s, 0.0) * w_h
        mb_ref[...] = acc
        return carry

    lax.fori_loop(0, IDX_HEADS // 2, idx_pair, 0)

    t_pos = lax.broadcasted_iota(jnp.int32, shape, 0) + t0
    s_pos = lax.broadcasted_iota(jnp.int32, shape, 1)
    causal = s_pos <= t_pos
    score = jnp.where(causal, mb_ref[...], -jnp.inf)
    bits = lax.bitcast_convert_type(score, jnp.int32)
    key_ref[...] = jnp.where(bits < 0, bits ^ jnp.int32(0x7FFFFFFF), bits)

    def bit_step(j, thr):
        cand = thr + lax.shift_left(jnp.int32(1), 31 - j)
        cnt = jnp.sum(jnp.where(key_ref[...] >= cand, 1.0, 0.0), axis=-1, keepdims=True)
        return jnp.where(cnt >= topk, cand, thr)

    thr = lax.fori_loop(0, 32, bit_step, jnp.full((tq, 1), INT_MIN, jnp.int32))

    key = key_ref[...]
    ge = key >= thr
    mb_ref[...] = jnp.where(ge & causal, 0.0, NEG_BIG)
    n_ge = jnp.sum(jnp.where(ge, 1.0, 0.0), axis=-1, keepdims=True)
    tied = jnp.where((n_ge > topk) & (thr > KEY_NEG_INF), 1.0, 0.0)

    @pl.when(jnp.max(tied) > 0.0)
    def _():
        kk = key_ref[...]
        gt = kk > thr
        tie = jnp.where(kk == thr, 1.0, 0.0)
        room = topk - jnp.sum(jnp.where(gt, 1.0, 0.0), axis=-1, keepdims=True)
        x = tie
        sh = 1
        while sh < sk:
            x = x + jnp.where(s_pos >= sh, pltpu.roll(x, sh, 1), 0.0)
            sh *= 2
        take = (tie > 0.0) & (x - tie < room)
        mb_ref[...] = jnp.where((gt | take) & causal, 0.0, NEG_BIG)

    lg_ref[:, :, 0:DSA_LEAD] = jnp.zeros((2, tq, DSA_LEAD), F32)
    off = pl.multiple_of(t0, LANE)

    def head_pair(hp, carry):
        cols = pl.ds(pl.multiple_of(hp * 2 * HEAD_DIM, 2 * HEAD_DIM), 2 * HEAD_DIM)
        q_pair = qa_ref[:, cols]
        q2 = jnp.concatenate([q_pair[:, :HEAD_DIM], q_pair[:, HEAD_DIM:]], axis=0)
        lg2 = lax.dot_general(q2, ka_ref[0:sk, :], _NT, preferred_element_type=F32)
        mb = mb_ref[...]
        ps, ls = [], []
        for t in range(2):
            lg_ref[t, :, DSA_LEAD:] = lg2[t * tq:(t + 1) * tq, :] + mb
            lg_ref[t, :, pl.ds(off, DSA_LEAD + tq)] += dc_ref[2 * hp + t]
            lg = lg_ref[t, :, DSA_LEAD:]
            p = jnp.exp2(lg - jnp.max(lg, axis=-1, keepdims=True))
            ls.append(jnp.sum(p, axis=-1, keepdims=True))
            ps.append(p.astype(BF16))
        o2 = jnp.dot(jnp.concatenate(ps, axis=0), va_ref[0:sk, :], preferred_element_type=F32)
        o_ref[0, :, cols] = jnp.concatenate(
            [o2[0:tq, :] / ls[0], o2[tq:2 * tq, :] / ls[1]], axis=1).astype(o_ref.dtype)
        return carry

    lax.fori_loop(0, N_HEADS // 2, head_pair, 0)


def _dsa_attention(proj, proj_small, bias_tiles, batch, seq):
    tq, grp = DSA_TQ, DSA_GROUP
    per_grp = grp // tq
    nq = seq // tq
    topk = min(TOPK_MAX, seq // 4)
    n_idx = IDX_HEADS * IDX_DIM
    outs = []
    for g in range(seq // grp):
        sk = (g + 1) * grp
        qrow = lambda b, j, g=g: b * nq + g * per_grp + j
        out = pl.pallas_call(
            functools.partial(_dsa_kernel, t_group=g * grp, sk=sk, topk=topk),
            grid=(batch, per_grp),
            in_specs=[pl.BlockSpec((tq, n_idx), lambda b, j, q=qrow: (q(b, j), OFF_QI // n_idx)),
                      pl.BlockSpec((seq, LANE), lambda b, j: (b, 0)),
                      pl.BlockSpec((seq, LANE), lambda b, j: (b, 1)),
                      pl.BlockSpec((tq, LANE), lambda b, j, q=qrow: (q(b, j), 2)),
                      pl.BlockSpec((tq, W_ATT), lambda b, j, q=qrow: (q(b, j), OFF_QA // W_ATT)),
                      pl.BlockSpec((seq, HEAD_DIM), lambda b, j: (b, OFF_KA // HEAD_DIM)),
                      pl.BlockSpec((seq, HEAD_DIM), lambda b, j: (b, OFF_VA // HEAD_DIM)),
                      pl.BlockSpec((N_HEADS, tq, DSA_LEAD + tq), lambda b, j: (0, 0, 0))],
            out_specs=pl.BlockSpec((1, tq, W_ATT), lambda b, j: (b, j, 0)),
            out_shape=jax.ShapeDtypeStruct((batch, grp, W_ATT), BF16),
            scratch_shapes=[pltpu.VMEM((tq, sk), jnp.int32),
                            pltpu.VMEM((tq, sk), F32),
                            pltpu.VMEM((2, tq, DSA_LEAD + sk), F32)],
            compiler_params=_params(("arbitrary", "arbitrary"), 48),
            name=f"dsa_attention_k{sk}",
        )(proj, proj_small, proj_small, proj_small, proj, proj, proj, bias_tiles)
        outs.append(out)
    return jnp.concatenate(outs, axis=1).reshape(batch * seq, W_ATT)


_SEG_SIZES = (W_ATT, HEAD_DIM, HEAD_DIM, IDX_HEADS * IDX_DIM, IDX_DIM, IDX_HEADS,
              W_ATT, W_ATT, W_ATT, N_HEADS, D_MODEL, D_MODEL)
(_C_QA, _C_KA, _C_VA, _C_QI, _C_KI, _C_WI, _C_QB, _C_KB, _C_VB, _C_FB, _C_GA, _C_GB) = (
    int(v) for v in np.concatenate([[0], np.cumsum(_SEG_SIZES)[:-1]]))


def _proj_blocks():
    def seg(start, width, is_q=0):
        return [(start + PROJ_BN * t, is_q) for t in range(width // PROJ_BN)]

    blocks = (seg(_C_GA, D_MODEL) + seg(_C_GB, D_MODEL) + seg(_C_QA, W_ATT, 1)
              + seg(_C_QB, W_ATT, 1) + seg(_C_KB, W_ATT) + seg(_C_VB, W_ATT)
              + seg(_C_QI, IDX_HEADS * IDX_DIM))
    assert len(blocks) * PROJ_BN == OFF_KA
    blocks.append((_C_KA, 0))
    assert len(blocks) * PROJ_BN == N_MAIN
    return blocks


def _proj_kernel(src_ref, isq_ref, a_ref, wt_hbm, o_ref, wbuf, wbf_ref, sem):
    j = pl.program_id(0)
    bn = wbf_ref.shape[0]

    def fetch(jj, slot):
        rows = pl.ds(pl.multiple_of(src_ref[jj], 8), bn)
        return pltpu.make_async_copy(wt_hbm.at[rows], wbuf.at[slot], sem.at[slot])

    slot = lax.rem(j, 2)
    first = pl.program_id(1) == 0

    @pl.when(first)
    def _():
        @pl.when(j == 0)
        def _():
            fetch(0, 0).start()

        fetch(j, slot).wait()

        @pl.when(j + 1 < pl.num_programs(0))
        def _():
            fetch(j + 1, 1 - slot).start()

        w = (wbuf[slot] * jnp.where(isq_ref[j] == 1, Q_SCALE, 1.0)).astype(BF16)
        wbf_ref[...] = w
        o_ref[...] = lax.dot_general(a_ref[...], w, _NT,
                                     preferred_element_type=F32).astype(o_ref.dtype)

    @pl.when(jnp.logical_not(first))
    def _():
        o_ref[...] = lax.dot_general(a_ref[...], wbf_ref[...], _NT,
                                     preferred_element_type=F32).astype(o_ref.dtype)


def _mixer_proj(h, w_t):
    m, k = h.shape
    bm, bn = 1024, PROJ_BN
    blocks = _proj_blocks()
    src = np.array([b[0] for b in blocks], np.int32)
    assert np.all(src % 8 == 0) and np.all(src + bn <= w_t.shape[0])
    is_q = np.array([b[1] for b in blocks], np.int32)
    nj = len(blocks)
    return pl.pallas_call(
        _proj_kernel,
        grid_spec=pltpu.PrefetchScalarGridSpec(
            num_scalar_prefetch=2,
            grid=(nj, m // bm),
            in_specs=[pl.BlockSpec((bm, k), lambda j, i, *_: (i, 0)),
                      pl.BlockSpec(memory_space=pl.ANY)],
            out_specs=pl.BlockSpec((bm, bn), lambda j, i, *_: (i, j)),
            scratch_shapes=[pltpu.VMEM((2, bn, k), F32),
                            pltpu.VMEM((bn, k), BF16),
                            pltpu.SemaphoreType.DMA((2,))]),
        out_shape=jax.ShapeDtypeStruct((m, nj * bn), BF16),
        compiler_params=_params(("arbitrary", "arbitrary"), 48),
        name="mixer_proj",
    )(jnp.asarray(src), jnp.asarray(is_q), h, w_t)


def _small_weight_t(w_t):
    k = w_t.shape[1]
    z = lambda n: jnp.zeros((n, k), w_t.dtype)
    k_i = w_t[_C_KI:_C_KI + IDX_DIM]
    return jnp.concatenate([k_i, z(LANE - IDX_DIM), z(LANE - IDX_DIM), k_i,
                            w_t[_C_WI:_C_WI + IDX_HEADS] * IDX_SCALE, z(LANE - IDX_HEADS),
                            w_t[_C_FB:_C_FB + N_HEADS], z(LANE - N_HEADS)], axis=0).astype(BF16)


def _nt_kernel(a_ref, bt_ref, o_ref):
    o_ref[...] = lax.dot_general(a_ref[...], bt_ref[...], _NT,
                                 preferred_element_type=F32).astype(o_ref.dtype)


def _matmul_nt(a, b_t, bm, out_dtype, vmem_mib, name):
    m, k = a.shape
    n = b_t.shape[0]
    return pl.pallas_call(
        _nt_kernel,
        grid=(m // bm,),
        in_specs=[pl.BlockSpec((bm, k), lambda i: (i, 0)),
                  pl.BlockSpec((n, k), lambda i: (0, 0))],
        out_specs=pl.BlockSpec((bm, n), lambda i: (i, 0)),
        out_shape=jax.ShapeDtypeStruct((m, n), out_dtype),
        compiler_params=_params(("arbitrary",), vmem_mib),
        name=name,
    )(a, b_t)


def _ffn(h, w_in, w_out):
    act, w_out_bf = _ffn_in(h, w_in, w_out)
    return _matmul(act, w_out_bf, 512, 512, BF16, 56, "ffn_out")


def kernel(x, c, w_ada, b_ada, g_ffn1, ffn1_w_in, ffn1_w_out, g_mix, w_in, b_forget, rel_bias,
           w_up_a, w_up_b, w_o, g_ffn2, ffn2_w_in, ffn2_w_out, g_final):
    batch, seq, d = x.shape
    assert w_ada.shape[0] == 1, "single-layer trunk"
    xf = x.reshape(batch * seq, d)

    c_pad = jnp.zeros((16, d), F32).at[:batch].set(c)
    bias_tiles = _bias_tiles(rel_bias, DSA_TQ, DSA_LEAD)
    mod = _ada(c_pad, w_ada[0], b_ada[0][None, :])[:batch]
    sh1, sc1, gt1, sh2, sc2, gt2, sh3, sc3, gt3 = [
        v.reshape(batch, 1, d) for v in jnp.split(mod, N_MOD, axis=-1)]

    (h,) = _norm(xf, g_ffn1[0][None, :], batch=batch, shift=sh1, scale=sc1)
    y = _ffn(h, ffn1_w_in[0], ffn1_w_out[0])

    xf, h = _norm(xf, g_mix[0][None, :], batch=batch, y=y, gate=gt1, y_scale=0.5,
                  shift=sh2, scale=sc2)
    w_t = jnp.swapaxes(w_in, 1, 2)[0]
    proj = _mixer_proj(h, w_t)
    proj_small = _matmul_nt(h, _small_weight_t(w_t), 1024, F32, 40, "mixer_proj_small")
    b_f = jnp.zeros((1, LANE), F32).at[0, :N_HEADS].set(b_forget[0])
    f_cum = _forget_cumsum(proj_small, b_f, batch, seq)
    o_b = _fox_attention(proj, f_cum, batch, seq)
    o_a = _dsa_attention(proj, proj_small, bias_tiles, batch, seq)
    merged = _merge(o_a, o_b, w_up_a[0], w_up_b[0], proj)
    y = _matmul_ws(merged, w_o[0], 1024, 512, BF16, 48, "mixer_out")

    xf, h = _norm(xf, g_ffn2[0][None, :], batch=batch, y=y, gate=gt2, y_scale=1.0,
                  shift=sh3, scale=sc3)
    y = _ffn(h, ffn2_w_in[0], ffn2_w_out[0])
    (out,) = _norm(xf, g_final[None, :], batch=batch, y=y, gate=gt3, y_scale=0.5)
    return out.reshape(batch, seq, d)
```

```python
import functools
import math

import numpy as np
import jax
import jax.numpy as jnp
from jax import lax
from jax.experimental import pallas as pl
from jax.experimental.pallas import tpu as pltpu

F32 = jnp.float32
BF16 = jnp.bfloat16

D_MODEL = 4096
HEAD_DIM = 128
N_HEADS = 16
W_ATT = N_HEADS * HEAD_DIM
IDX_HEADS = 16
IDX_DIM = 64
TOPK_MAX = 256
N_BUCKETS = 32
MAX_DISTANCE = 128
D_FF = 11008
N_MOD = 9
RMS_EPS = 1e-6
LOG2E = math.log2(math.e)
Q_SCALE = HEAD_DIM ** -0.5 * LOG2E
IDX_SCALE = IDX_HEADS ** -0.5 * IDX_DIM ** -0.5

LANE = 128
MIB = 1024 * 1024
NEG_BIG = -1e30
INT_MIN = -(2 ** 31)
KEY_NEG_INF = (0xFF800000 ^ 0x7FFFFFFF) - 2 ** 32

OFF_GATE_A = 0
OFF_GATE_B = OFF_GATE_A + D_MODEL
OFF_QA = OFF_GATE_B + D_MODEL
OFF_QB = OFF_QA + W_ATT
OFF_KB = OFF_QB + W_ATT
OFF_VB = OFF_KB + W_ATT
OFF_QI = OFF_VB + W_ATT
OFF_KA = OFF_QI + IDX_HEADS * IDX_DIM
OFF_VA = OFF_KA + HEAD_DIM
PROJ_BN = 512
N_MAIN = OFF_KA + PROJ_BN
N_SMALL = 4 * LANE

DSA_TQ = 256
DSA_GROUP = 512
DSA_LEAD = LANE
DSA_UNROLL = 4


def _params(semantics, vmem_mib):
    return pltpu.CompilerParams(dimension_semantics=semantics,
                                vmem_limit_bytes=vmem_mib * MIB)


def _ada_kernel(c_ref, w_ref, b_ref, o_ref):
    c = c_ref[...]
    ca = (c * jax.nn.sigmoid(c)).astype(BF16)
    o_ref[...] = jnp.dot(ca, w_ref[...].astype(BF16), preferred_element_type=F32) + b_ref[...]


def _ada(c_pad, w, b):
    rows, d = c_pad.shape
    n = w.shape[1]
    bn = 512
    return pl.pallas_call(
        _ada_kernel,
        grid=(n // bn,),
        in_specs=[pl.BlockSpec((rows, d), lambda j: (0, 0)),
                  pl.BlockSpec((d, bn), lambda j: (0, j)),
                  pl.BlockSpec((1, bn), lambda j: (0, j))],
        out_specs=pl.BlockSpec((rows, bn), lambda j: (0, j)),
        out_shape=jax.ShapeDtypeStruct((rows, n), F32),
        compiler_params=_params(("arbitrary",), 40),
        name="ada_mod",
    )(c_pad, w, b)


def _norm_kernel(*refs, has_y, y_scale, modulated):
    it = iter(refs)
    x_ref = next(it)
    if has_y:
        y_ref = next(it)
        gate_ref = next(it)
    g_ref = next(it)
    if modulated:
        shift_ref = next(it)
        scale_ref = next(it)
    x = x_ref[...]
    if has_y:
        x = x + (y_scale * gate_ref[0]) * y_ref[...].astype(F32)
        xo_ref = next(it)
        if modulated:
            xo_ref[...] = x
    ms = jnp.mean(x * x, axis=-1, keepdims=True)
    nrm = x * lax.rsqrt(ms + RMS_EPS) * g_ref[...]
    if modulated:
        h_ref = next(it)
        h_ref[...] = (nrm * (1.0 + scale_ref[0]) + shift_ref[0]).astype(h_ref.dtype)
    else:
        xo_ref[...] = nrm


def _norm(x, g, *, batch, y=None, gate=None, y_scale=1.0, shift=None, scale=None):
    m, d = x.shape
    rows = 256
    per_b = m // batch // rows
    has_y = y is not None
    modulated = shift is not None
    row_spec = pl.BlockSpec((rows, d), lambda b, i: (b * per_b + i, 0))
    vec_spec = pl.BlockSpec((1, 1, d), lambda b, i: (b, 0, 0))
    ins, specs = [x], [row_spec]
    if has_y:
        ins += [y, gate]
        specs += [row_spec, vec_spec]
    ins.append(g)
    specs.append(pl.BlockSpec((1, d), lambda b, i: (0, 0)))
    if modulated:
        ins += [shift, scale]
        specs += [vec_spec, vec_spec]
    outs, out_specs = [], []
    if has_y:
        outs.append(jax.ShapeDtypeStruct((m, d), F32))
        out_specs.append(row_spec)
    if modulated:
        outs.append(jax.ShapeDtypeStruct((m, d), BF16))
        out_specs.append(row_spec)
    return pl.pallas_call(
        functools.partial(_norm_kernel, has_y=has_y, y_scale=y_scale, modulated=modulated),
        grid=(batch, per_b),
        in_specs=specs,
        out_specs=out_specs,
        out_shape=outs,
        compiler_params=_params(("arbitrary", "arbitrary"), 48),
        name="resid_norm",
    )(*ins)


def _mm_kernel(a_ref, b_ref, o_ref):
    o_ref[...] = jnp.dot(a_ref[...], b_ref[...],
                         preferred_element_type=F32).astype(o_ref.dtype)


def _matmul(a, b, bm, bn, out_dtype, vmem_mib, name):
    m, k = a.shape
    n = b.shape[1]
    return pl.pallas_call(
        _mm_kernel,
        grid=(m // bm, n // bn),
        in_specs=[pl.BlockSpec((bm, k), lambda i, j: (i, 0)),
                  pl.BlockSpec((k, bn), lambda i, j: (0, j))],
        out_specs=pl.BlockSpec((bm, bn), lambda i, j: (i, j)),
        out_shape=jax.ShapeDtypeStruct((m, n), out_dtype),
        compiler_params=_params(("arbitrary", "arbitrary"), vmem_mib),
        name=name,
    )(a, b)


def _with_stationary_weights(compute, sources, wbuf, wbf_ref, sem, prep=None):
    j = pl.program_id(0)
    n_w = wbf_ref.shape[0]

    def copies(jj, slot):
        return [pltpu.make_async_copy(src, wbuf.at[slot, p], sem.at[slot, p])
                for p, src in enumerate(sources(jj))]

    slot = lax.rem(j, 2)
    first = pl.program_id(1) == 0

    @pl.when(first)
    def _():
        @pl.when(j == 0)
        def _():
            for c in copies(0, 0):
                c.start()

        for c in copies(j, slot):
            c.wait()

        @pl.when(j + 1 < pl.num_programs(0))
        def _():
            for c in copies(j + 1, 1 - slot):
                c.start()

        ws = []
        for p in range(n_w):
            w = wbuf[slot, p]
            ws.append((w if prep is None else prep(w)).astype(BF16))
            wbf_ref[p] = ws[p]
        compute(ws)

    @pl.when(jnp.logical_not(first))
    def _():
        compute([wbf_ref[p] for p in range(n_w)])


def _ws_scratch(n_w, rows, cols):
    return [pltpu.VMEM((2, n_w, rows, cols), F32), pltpu.VMEM((n_w, rows, cols), BF16),
            pltpu.SemaphoreType.DMA((2, n_w))]


def _col_block(w_hbm, block, bn):
    return w_hbm.at[:, pl.ds(pl.multiple_of(block * bn, bn), bn)]


def _mm_ws_kernel(a_ref, w_hbm, o_ref, wbuf, wbf_ref, sem):
    bn = o_ref.shape[1]

    def compute(ws):
        o_ref[...] = jnp.dot(a_ref[...], ws[0], preferred_element_type=F32).astype(o_ref.dtype)

    _with_stationary_weights(compute, lambda jj: [_col_block(w_hbm, jj, bn)], wbuf, wbf_ref, sem)


def _matmul_ws(a, w, bm, bn, out_dtype, vmem_mib, name):
    m, k = a.shape
    n = w.shape[1]
    return pl.pallas_call(
        _mm_ws_kernel,
        grid=(n // bn, m // bm),
        in_specs=[pl.BlockSpec((bm, k), lambda j, i: (i, 0)),
                  pl.BlockSpec(memory_space=pl.ANY)],
        out_specs=pl.BlockSpec((bm, bn), lambda j, i: (i, j)),
        out_shape=jax.ShapeDtypeStruct((m, n), out_dtype),
        scratch_shapes=_ws_scratch(1, k, bn),
        compiler_params=_params(("arbitrary", "arbitrary"), vmem_mib),
        name=name,
    )(a, w)


def _ffn_in_kernel(h_ref, w_hbm, wo_ref, o_ref, wo_bf_ref, wbuf, wbf_ref, sem):
    bn = o_ref.shape[1]
    nb = pl.num_programs(0)

    def compute(ws):
        h = h_ref[...]
        a = jnp.dot(h, ws[0], preferred_element_type=F32)
        b = jnp.dot(h, ws[1], preferred_element_type=F32)
        o_ref[...] = (a * jax.nn.sigmoid(a) * b).astype(o_ref.dtype)

    _with_stationary_weights(
        compute, lambda jj: [_col_block(w_hbm, jj, bn), _col_block(w_hbm, jj + nb, bn)],
        wbuf, wbf_ref, sem)
    wo_bf_ref[...] = wo_ref[...].astype(BF16)


def _ffn_in(h, w_in, w_out):
    m, k = h.shape
    bm, bn = 1024, 256
    nb, mb = D_FF // bn, m // bm
    wo_rows = D_FF // (nb * mb)
    assert wo_rows * nb * mb == D_FF and w_out.shape[0] == D_FF
    wo_spec = pl.BlockSpec((wo_rows, w_out.shape[1]), lambda j, i: (j * mb + i, 0))
    return pl.pallas_call(
        _ffn_in_kernel,
        grid=(nb, mb),
        in_specs=[pl.BlockSpec((bm, k), lambda j, i: (i, 0)),
                  pl.BlockSpec(memory_space=pl.ANY),
                  wo_spec],
        out_specs=[pl.BlockSpec((bm, bn), lambda j, i: (i, j)), wo_spec],
        out_shape=[jax.ShapeDtypeStruct((m, D_FF), BF16),
                   jax.ShapeDtypeStruct(w_out.shape, BF16)],
        scratch_shapes=_ws_scratch(2, k, bn),
        compiler_params=_params(("arbitrary", "arbitrary"), 48),
        name="ffn_in_swiglu",
    )(h, w_in, w_out)


def _merge_kernel(oa_ref, ob_ref, wa_hbm, wb_hbm, ga_ref, gb_ref, o_ref, wbuf, wbf_ref, sem):
    bn = o_ref.shape[1]

    def compute(ws):
        ya = jnp.dot(oa_ref[...], ws[0], preferred_element_type=F32)
        yb = jnp.dot(ob_ref[...], ws[1], preferred_element_type=F32)
        ga = jax.nn.sigmoid(ga_ref[...].astype(F32))
        gb = jax.nn.sigmoid(gb_ref[...].astype(F32))
        o_ref[...] = (ga * ya + gb * yb).astype(o_ref.dtype)

    _with_stationary_weights(
        compute, lambda jj: [_col_block(wa_hbm, jj, bn), _col_block(wb_hbm, jj, bn)],
        wbuf, wbf_ref, sem)


def _merge(o_a, o_b, w_up_a, w_up_b, proj):
    m, k = o_a.shape
    bm, bn = 1024, 512
    gb_off = OFF_GATE_B // bn
    return pl.pallas_call(
        _merge_kernel,
        grid=(D_MODEL // bn, m // bm),
        in_specs=[pl.BlockSpec((bm, k), lambda j, i: (i, 0)),
                  pl.BlockSpec((bm, k), lambda j, i: (i, 0)),
                  pl.BlockSpec(memory_space=pl.ANY),
                  pl.BlockSpec(memory_space=pl.ANY),
                  pl.BlockSpec((bm, bn), lambda j, i: (i, j)),
                  pl.BlockSpec((bm, bn), lambda j, i: (i, j + gb_off))],
        out_specs=pl.BlockSpec((bm, bn), lambda j, i: (i, j)),
        out_shape=jax.ShapeDtypeStruct((m, D_MODEL), BF16),
        scratch_shapes=_ws_scratch(2, k, bn),
        compiler_params=_params(("arbitrary", "arbitrary"), 48),
        name="gated_merge",
    )(o_a, o_b, w_up_a, w_up_b, proj, proj)


def _t5_bucket_np(dist):
    n = np.maximum(dist, 0)
    max_exact = N_BUCKETS // 2
    nf = np.maximum(n, max_exact).astype(np.float32)
    large = max_exact + (np.log(nf / max_exact) / math.log(MAX_DISTANCE / max_exact)
                         * (N_BUCKETS - max_exact)).astype(np.int32)
    large = np.minimum(large, N_BUCKETS - 1)
    return np.where(n < max_exact, n, large).astype(np.int32)


def _bias_tile_kernel(rb_ref, bk_ref, o_ref):
    h = pl.program_id(0)
    bk = bk_ref[...]
    far = rb_ref[N_BUCKETS - 1, h]
    acc = jnp.zeros(bk.shape, F32)
    for k in range(N_BUCKETS - 1):
        acc = jnp.where(bk == k, (rb_ref[k, h] - far) * LOG2E, acc)
    o_ref[0] = acc


def _bias_tiles(rel_bias, rows, lead):
    cols = lead + rows
    dist = lead + np.arange(rows)[:, None] - np.arange(cols)[None, :]
    bucket = np.where(dist >= 0, _t5_bucket_np(dist), N_BUCKETS - 1).astype(np.int32)
    assert _t5_bucket_np(np.array([lead + 1]))[0] == N_BUCKETS - 1
    return pl.pallas_call(
        _bias_tile_kernel,
        grid=(N_HEADS,),
        in_specs=[pl.BlockSpec(memory_space=pltpu.SMEM),
                  pl.BlockSpec((rows, cols), lambda h: (0, 0))],
        out_specs=pl.BlockSpec((1, rows, cols), lambda h: (h, 0, 0)),
        out_shape=jax.ShapeDtypeStruct((N_HEADS, rows, cols), F32),
        compiler_params=_params(("arbitrary",), 16),
        name="t5_bias_tiles",
    )(rel_bias, jnp.asarray(bucket))


def _forget_kernel(f_ref, b_ref, o_ref):
    z = f_ref[...] + b_ref[...]
    ls = jnp.minimum(z, 0.0) - jnp.log(1.0 + jnp.exp(-jnp.abs(z)))
    x = ls.T[0:N_HEADS, :]
    seq = x.shape[1]
    lane = lax.broadcasted_iota(jnp.int32, x.shape, 1)
    sh = 1
    while sh < seq:
        x = x + jnp.where(lane >= sh, pltpu.roll(x, sh, 1), 0.0)
        sh *= 2
    o_ref[0] = x * LOG2E


def _forget_cumsum(proj_small, b_forget_pad, batch, seq):
    return pl.pallas_call(
        _forget_kernel,
        grid=(batch,),
        in_specs=[pl.BlockSpec((seq, LANE), lambda b: (b, 3)),
                  pl.BlockSpec((1, LANE), lambda b: (0, 0))],
        out_specs=pl.BlockSpec((1, N_HEADS, seq), lambda b: (b, 0, 0)),
        out_shape=jax.ShapeDtypeStruct((batch, N_HEADS, seq), F32),
        compiler_params=_params(("arbitrary",), 32),
        name="forget_cumsum",
    )(proj_small, b_forget_pad)


_NT = (((1,), (1,)), ((), ()))


def _fox_kernel(q_ref, k_ref, v_ref, f_ref, o_ref, *, tq):
    h = pl.program_id(1)
    seq = q_ref.shape[0]
    frow = f_ref[0, pl.ds(h, 1), :]
    row = lax.broadcasted_iota(jnp.int32, (tq, tq), 0)
    col = lax.broadcasted_iota(jnp.int32, (tq, tq), 1)
    diag_mask = jnp.where(col <= row, 0.0, NEG_BIG)
    for i in range(seq // tq):
        lo, hi = i * tq, (i + 1) * tq
        q = q_ref[lo:hi, :]
        lg_d = (lax.dot_general(q, k_ref[lo:hi, :], _NT, preferred_element_type=F32)
                - frow[:, lo:hi] + diag_mask)
        m = jnp.max(lg_d, axis=-1, keepdims=True)
        if i > 0:
            lg_p = (lax.dot_general(q, k_ref[0:lo, :], _NT, preferred_element_type=F32)
                    - frow[:, 0:lo])
            m = jnp.maximum(m, jnp.max(lg_p, axis=-1, keepdims=True))
        p_d = jnp.exp2(lg_d - m)
        l = jnp.sum(p_d, axis=-1, keepdims=True)
        o = jnp.dot(p_d.astype(BF16), v_ref[lo:hi, :], preferred_element_type=F32)
        if i > 0:
            p_p = jnp.exp2(lg_p - m)
            l = l + jnp.sum(p_p, axis=-1, keepdims=True)
            o = o + jnp.dot(p_p.astype(BF16), v_ref[0:lo, :], preferred_element_type=F32)
        o_ref[lo:hi, :] = (o / l).astype(o_ref.dtype)


def _fox_attention(proj, f_cum, batch, seq):
    tq = 256
    qc, kc, vc = OFF_QB // HEAD_DIM, OFF_KB // HEAD_DIM, OFF_VB // HEAD_DIM
    blk = (seq, HEAD_DIM)
    return pl.pallas_call(
        functools.partial(_fox_kernel, tq=tq),
        grid=(batch, N_HEADS),
        in_specs=[pl.BlockSpec(blk, lambda b, h: (b, qc + h)),
                  pl.BlockSpec(blk, lambda b, h: (b, kc + h)),
                  pl.BlockSpec(blk, lambda b, h: (b, vc + h)),
                  pl.BlockSpec((1, N_HEADS, seq), lambda b, h: (b, 0, 0))],
        out_specs=pl.BlockSpec(blk, lambda b, h: (b, h)),
        out_shape=jax.ShapeDtypeStruct((batch * seq, W_ATT), BF16),
        compiler_params=_params(("arbitrary", "arbitrary"), 32),
        name="fox_attention",
    )(proj, proj, proj, f_cum)


def _dsa_kernel(qi_ref, k0_ref, k1_ref, w_ref, qa_ref, ka_ref, va_ref, dc_ref, o_ref,
                key_ref, mb_ref, lg_ref, *, t_group, sk, topk):
    tq = qa_ref.shape[0]
    t0 = t_group + pl.program_id(1) * tq
    shape = (tq, sk)

    w_lane = lax.broadcasted_iota(jnp.int32, (tq, LANE), 1)
    mb_ref[...] = jnp.zeros(shape, F32)

    def idx_pair(hp, carry):
        q2 = qi_ref[:, pl.ds(pl.multiple_of(hp * LANE, LANE), LANE)]
        w = w_ref[...]
        acc = mb_ref[...]
        for par, k_ref in ((0, k0_ref), (1, k1_ref)):
            s = lax.dot_general(q2, k_ref[0:sk, :].astype(BF16), _NT, preferred_element_type=F32)
            w_h = jnp.sum(jnp.where(w_lane == 2 * hp + par, w, 0.0), axis=-1, keepdims=True)
            acc = acc + jnp.maximum(s, 0.0) * w_h
        mb_ref[...] = acc
        return carry

    lax.fori_loop(0, IDX_HEADS // 2, idx_pair, 0)

    t_pos = lax.broadcasted_iota(jnp.int32, shape, 0) + t0
    s_pos = lax.broadcasted_iota(jnp.int32, shape, 1)
    causal = s_pos <= t_pos
    score = jnp.where(causal, mb_ref[...], -jnp.inf)
    bits = lax.bitcast_convert_type(score, jnp.int32)
    key_ref[...] = jnp.where(bits < 0, bits ^ jnp.int32(0x7FFFFFFF), bits)

    def bit_step(j, thr):
        cand = thr + lax.shift_left(jnp.int32(1), 31 - j)
        cnt = jnp.sum(jnp.where(key_ref[...] >= cand, 1.0, 0.0), axis=-1, keepdims=True)
        return jnp.where(cnt >= topk, cand, thr)

    thr = lax.fori_loop(0, 32, bit_step, jnp.full((tq, 1), INT_MIN, jnp.int32))

    key = key_ref[...]
    ge = key >= thr
    mb_ref[...] = jnp.where(ge & causal, 0.0, NEG_BIG)
    n_ge = jnp.sum(jnp.where(ge, 1.0, 0.0), axis=-1, keepdims=True)
    tied = jnp.where((n_ge > topk) & (thr > KEY_NEG_INF), 1.0, 0.0)

    @pl.when(jnp.max(tied) > 0.0)
    def _():
        kk = key_ref[...]
        gt = kk > thr
        tie = jnp.where(kk == thr, 1.0, 0.0)
        room = topk - jnp.sum(jnp.where(gt, 1.0, 0.0), axis=-1, keepdims=True)
        x = tie
        sh = 1
        while sh < sk:
            x = x + jnp.where(s_pos >= sh, pltpu.roll(x, sh, 1), 0.0)
            sh *= 2
        take = (tie > 0.0) & (x - tie < room)
        mb_ref[...] = jnp.where((gt | take) & causal, 0.0, NEG_BIG)

    lg_ref[:, :, 0:DSA_LEAD] = jnp.zeros((DSA_UNROLL, tq, DSA_LEAD), F32)
    off = pl.multiple_of(t0, LANE)

    def head_group(g, carry):
        for u in range(DSA_UNROLL):
            h = g * DSA_UNROLL + u
            hcol = pl.ds(pl.multiple_of(h * HEAD_DIM, HEAD_DIM), HEAD_DIM)
            lg_ref[u, :, DSA_LEAD:] = (
                lax.dot_general(qa_ref[:, hcol], ka_ref[0:sk, :], _NT, preferred_element_type=F32)
                + mb_ref[...])
            lg_ref[u, :, pl.ds(off, DSA_LEAD + tq)] += dc_ref[h]
            lg = lg_ref[u, :, DSA_LEAD:]
            p = jnp.exp2(lg - jnp.max(lg, axis=-1, keepdims=True))
            l = jnp.sum(p, axis=-1, keepdims=True)
            o = jnp.dot(p.astype(BF16), va_ref[0:sk, :], preferred_element_type=F32)
            o_ref[0, :, hcol] = (o / l).astype(o_ref.dtype)
        return carry

    lax.fori_loop(0, N_HEADS // DSA_UNROLL, head_group, 0)


def _dsa_attention(proj, proj_small, bias_tiles, batch, seq):
    tq, grp = DSA_TQ, DSA_GROUP
    per_grp = grp // tq
    nq = seq // tq
    topk = min(TOPK_MAX, seq // 4)
    n_idx = IDX_HEADS * IDX_DIM
    outs = []
    for g in range(seq // grp):
        sk = (g + 1) * grp
        qrow = lambda b, j, g=g: b * nq + g * per_grp + j
        out = pl.pallas_call(
            functools.partial(_dsa_kernel, t_group=g * grp, sk=sk, topk=topk),
            grid=(batch, per_grp),
            in_specs=[pl.BlockSpec((tq, n_idx), lambda b, j, q=qrow: (q(b, j), OFF_QI // n_idx)),
                      pl.BlockSpec((seq, LANE), lambda b, j: (b, 0)),
                      pl.BlockSpec((seq, LANE), lambda b, j: (b, 1)),
                      pl.BlockSpec((tq, LANE), lambda b, j, q=qrow: (q(b, j), 2)),
                      pl.BlockSpec((tq, W_ATT), lambda b, j, q=qrow: (q(b, j), OFF_QA // W_ATT)),
                      pl.BlockSpec((seq, HEAD_DIM), lambda b, j: (b, OFF_KA // HEAD_DIM)),
                      pl.BlockSpec((seq, HEAD_DIM), lambda b, j: (b, OFF_VA // HEAD_DIM)),
                      pl.BlockSpec((N_HEADS, tq, DSA_LEAD + tq), lambda b, j: (0, 0, 0))],
            out_specs=pl.BlockSpec((1, tq, W_ATT), lambda b, j: (b, j, 0)),
            out_shape=jax.ShapeDtypeStruct((batch, grp, W_ATT), BF16),
            scratch_shapes=[pltpu.VMEM((tq, sk), jnp.int32),
                            pltpu.VMEM((tq, sk), F32),
                            pltpu.VMEM((DSA_UNROLL, tq, DSA_LEAD + sk), F32)],
            compiler_params=_params(("arbitrary", "arbitrary"), 48),
            name=f"dsa_attention_k{sk}",
        )(proj, proj_small, proj_small, proj_small, proj, proj, proj, bias_tiles)
        outs.append(out)
    return jnp.concatenate(outs, axis=1).reshape(batch * seq, W_ATT)


_SEG_SIZES = (W_ATT, HEAD_DIM, HEAD_DIM, IDX_HEADS * IDX_DIM, IDX_DIM, IDX_HEADS,
              W_ATT, W_ATT, W_ATT, N_HEADS, D_MODEL, D_MODEL)
(_C_QA, _C_KA, _C_VA, _C_QI, _C_KI, _C_WI, _C_QB, _C_KB, _C_VB, _C_FB, _C_GA, _C_GB) = (
    int(v) for v in np.concatenate([[0], np.cumsum(_SEG_SIZES)[:-1]]))


def _proj_blocks():
    def seg(start, width, is_q=0):
        return [(start + PROJ_BN * t, is_q) for t in range(width // PROJ_BN)]

    blocks = (seg(_C_GA, D_MODEL) + seg(_C_GB, D_MODEL) + seg(_C_QA, W_ATT, 1)
              + seg(_C_QB, W_ATT, 1) + seg(_C_KB, W_ATT) + seg(_C_VB, W_ATT)
              + seg(_C_QI, IDX_HEADS * IDX_DIM))
    assert len(blocks) * PROJ_BN == OFF_KA
    blocks.append((_C_KA, 0))
    assert len(blocks) * PROJ_BN == N_MAIN
    return blocks


def _proj_kernel(src_ref, isq_ref, a_ref, wt_hbm, o_ref, wbuf, wbf_ref, sem):
    bn = o_ref.shape[1]
    scale = jnp.where(isq_ref[pl.program_id(0)] == 1, Q_SCALE, 1.0)

    def compute(ws):
        o_ref[...] = lax.dot_general(a_ref[...], ws[0], _NT,
                                     preferred_element_type=F32).astype(o_ref.dtype)

    _with_stationary_weights(
        compute, lambda jj: [wt_hbm.at[pl.ds(pl.multiple_of(src_ref[jj], 8), bn)]],
        wbuf, wbf_ref, sem, prep=lambda w: w * scale)


def _mixer_proj(h, w_t):
    m, k = h.shape
    bm, bn = 1024, PROJ_BN
    blocks = _proj_blocks()
    src = np.array([b[0] for b in blocks], np.int32)
    assert np.all(src % 8 == 0) and np.all(src + bn <= w_t.shape[0])
    is_q = np.array([b[1] for b in blocks], np.int32)
    nj = len(blocks)
    return pl.pallas_call(
        _proj_kernel,
        grid_spec=pltpu.PrefetchScalarGridSpec(
            num_scalar_prefetch=2,
            grid=(nj, m // bm),
            in_specs=[pl.BlockSpec((bm, k), lambda j, i, *_: (i, 0)),
                      pl.BlockSpec(memory_space=pl.ANY)],
            out_specs=pl.BlockSpec((bm, bn), lambda j, i, *_: (i, j)),
            scratch_shapes=_ws_scratch(1, bn, k)),
        out_shape=jax.ShapeDtypeStruct((m, nj * bn), BF16),
        compiler_params=_params(("arbitrary", "arbitrary"), 48),
        name="mixer_proj",
    )(jnp.asarray(src), jnp.asarray(is_q), h, w_t)


def _small_weight_t(w_t):
    k = w_t.shape[1]
    z = lambda n: jnp.zeros((n, k), w_t.dtype)
    k_i = w_t[_C_KI:_C_KI + IDX_DIM]
    return jnp.concatenate([k_i, z(LANE - IDX_DIM), z(LANE - IDX_DIM), k_i,
                            w_t[_C_WI:_C_WI + IDX_HEADS] * IDX_SCALE, z(LANE - IDX_HEADS),
                            w_t[_C_FB:_C_FB + N_HEADS], z(LANE - N_HEADS)], axis=0).astype(BF16)


def _nt_kernel(a_ref, bt_ref, o_ref):
    o_ref[...] = lax.dot_general(a_ref[...], bt_ref[...], _NT,
                                 preferred_element_type=F32).astype(o_ref.dtype)


def _matmul_nt(a, b_t, bm, out_dtype, vmem_mib, name):
    m, k = a.shape
    n = b_t.shape[0]
    return pl.pallas_call(
        _nt_kernel,
        grid=(m // bm,),
        in_specs=[pl.BlockSpec((bm, k), lambda i: (i, 0)),
                  pl.BlockSpec((n, k), lambda i: (0, 0))],
        out_specs=pl.BlockSpec((bm, n), lambda i: (i, 0)),
        out_shape=jax.ShapeDtypeStruct((m, n), out_dtype),
        compiler_params=_params(("arbitrary",), vmem_mib),
        name=name,
    )(a, b_t)


def _ffn(h, w_in, w_out):
    act, w_out_bf = _ffn_in(h, w_in, w_out)
    return _matmul(act, w_out_bf, 512, 512, BF16, 56, "ffn_out")


def kernel(x, c, w_ada, b_ada, g_ffn1, ffn1_w_in, ffn1_w_out, g_mix, w_in, b_forget, rel_bias,
           w_up_a, w_up_b, w_o, g_ffn2, ffn2_w_in, ffn2_w_out, g_final):
    batch, seq, d = x.shape
    assert w_ada.shape[0] == 1, "single-layer trunk"
    xf = x.reshape(batch * seq, d)

    c_pad = jnp.zeros((16, d), F32).at[:batch].set(c)
    bias_tiles = _bias_tiles(rel_bias, DSA_TQ, DSA_LEAD)
    mod = _ada(c_pad, w_ada[0], b_ada[0][None, :])[:batch]
    sh1, sc1, gt1, sh2, sc2, gt2, sh3, sc3, gt3 = [
        v.reshape(batch, 1, d) for v in jnp.split(mod, N_MOD, axis=-1)]

    (h,) = _norm(xf, g_ffn1[0][None, :], batch=batch, shift=sh1, scale=sc1)
    y = _ffn(h, ffn1_w_in[0], ffn1_w_out[0])

    xf, h = _norm(xf, g_mix[0][None, :], batch=batch, y=y, gate=gt1, y_scale=0.5,
                  shift=sh2, scale=sc2)
    w_t = jnp.swapaxes(w_in, 1, 2)[0]
    proj = _mixer_proj(h, w_t)
    proj_small = _matmul_nt(h, _small_weight_t(w_t), 1024, F32, 40, "mixer_proj_small")
    b_f = jnp.zeros((1, LANE), F32).at[0, :N_HEADS].set(b_forget[0])
    f_cum = _forget_cumsum(proj_small, b_f, batch, seq)
    o_b = _fox_attention(proj, f_cum, batch, seq)
    o_a = _dsa_attention(proj, proj_small, bias_tiles, batch, seq)
    merged = _merge(o_a, o_b, w_up_a[0], w_up_b[0], proj)
    y = _matmul_ws(merged, w_o[0], 1024, 512, BF16, 48, "mixer_out")

    xf, h = _norm(xf, g_ffn2[0][None, :], batch=batch, y=y, gate=gt2, y_scale=1.0,
                  shift=sh3, scale=sc3)
    y = _ffn(h, ffn2_w_in[0], ffn2_w_out[0])
    (out,) = _norm(xf, g_final[None, :], batch=batch, y=y, gate=gt3, y_scale=0.5)
    return out.reshape(batch, seq, d)
```

```python
import functools
import math

import numpy as np
import jax
import jax.numpy as jnp
from jax import lax
from jax.experimental import pallas as pl
from jax.experimental.pallas import tpu as pltpu

F32 = jnp.float32
BF16 = jnp.bfloat16

D_MODEL = 4096
HEAD_DIM = 128
N_HEADS = 16
W_ATT = N_HEADS * HEAD_DIM
IDX_HEADS = 16
IDX_DIM = 64
TOPK_MAX = 256
N_BUCKETS = 32
MAX_DISTANCE = 128
D_FF = 11008
N_MOD = 9
RMS_EPS = 1e-6
LOG2E = math.log2(math.e)
Q_SCALE = HEAD_DIM ** -0.5 * LOG2E
IDX_SCALE = IDX_HEADS ** -0.5 * IDX_DIM ** -0.5

LANE = 128
MIB = 1024 * 1024
NEG_BIG = -1e30
INT_MIN = -(2 ** 31)
KEY_NEG_INF = (0xFF800000 ^ 0x7FFFFFFF) - 2 ** 32

OFF_GATE_A = 0
OFF_GATE_B = OFF_GATE_A + D_MODEL
OFF_QA = OFF_GATE_B + D_MODEL
OFF_QB = OFF_QA + W_ATT
OFF_KB = OFF_QB + W_ATT
OFF_VB = OFF_KB + W_ATT
OFF_QI = OFF_VB + W_ATT
OFF_KA = OFF_QI + IDX_HEADS * IDX_DIM
OFF_VA = OFF_KA + HEAD_DIM
PROJ_BN = 512
N_MAIN = OFF_KA + PROJ_BN
N_SMALL = 4 * LANE

DSA_TQ = 256
DSA_GROUP = 512
DSA_LEAD = LANE


def _params(semantics, vmem_mib):
    return pltpu.CompilerParams(dimension_semantics=semantics,
                                vmem_limit_bytes=vmem_mib * MIB)


def _ada_kernel(c_ref, w_ref, b_ref, o_ref):
    c = c_ref[...]
    ca = (c * jax.nn.sigmoid(c)).astype(BF16)
    o_ref[...] = jnp.dot(ca, w_ref[...].astype(BF16), preferred_element_type=F32) + b_ref[...]


def _ada(c_pad, w, b, n):
    rows, d = c_pad.shape
    bn = 512
    return pl.pallas_call(
        _ada_kernel,
        grid=(n // bn,),
        in_specs=[pl.BlockSpec((rows, d), lambda j: (0, 0)),
                  pl.BlockSpec((d, bn), lambda j: (0, j)),
                  pl.BlockSpec((1, bn), lambda j: (0, j))],
        out_specs=pl.BlockSpec((rows, bn), lambda j: (0, j)),
        out_shape=jax.ShapeDtypeStruct((rows, n), F32),
        compiler_params=_params(("arbitrary",), 40),
        name="ada_mod",
    )(c_pad, w, b)


def _ada_bcast_kernel(c_ref, o_ref):
    c = c_ref[...]
    ct = (c * jax.nn.sigmoid(c)).T
    for b in range(o_ref.shape[0]):
        o_ref[b] = jnp.broadcast_to(ct[:, b:b + 1], o_ref.shape[1:])


def _ada_bcast(c_lanes, batch):
    rows, d = c_lanes.shape
    return pl.pallas_call(
        _ada_bcast_kernel,
        grid=(1,),
        in_specs=[pl.BlockSpec((rows, d), lambda j: (0, 0))],
        out_specs=pl.BlockSpec((batch, d, LANE), lambda j: (0, 0, 0)),
        out_shape=jax.ShapeDtypeStruct((batch, d, LANE), F32),
        compiler_params=_params(("arbitrary",), 40),
        name="ada_bcast",
    )(c_lanes)


def _norm_kernel(*refs, has_y, y_scale, modulated):
    it = iter(refs)
    x_ref = next(it)
    if has_y:
        y_ref = next(it)
        gate_ref = next(it)
    g_ref = next(it)
    if modulated:
        shift_ref = next(it)
        scale_ref = next(it)
    x = x_ref[...]
    if has_y:
        x = x + (y_scale * gate_ref[0]) * y_ref[...].astype(F32)
        xo_ref = next(it)
        if modulated:
            xo_ref[...] = x
    ms = jnp.mean(x * x, axis=-1, keepdims=True)
    nrm = x * lax.rsqrt(ms + RMS_EPS) * g_ref[...]
    if modulated:
        h_ref = next(it)
        h_ref[...] = (nrm * (1.0 + scale_ref[0]) + shift_ref[0]).astype(h_ref.dtype)
    else:
        xo_ref[...] = nrm


def _norm(x, g, *, batch, y=None, gate=None, y_scale=1.0, shift=None, scale=None):
    m, d = x.shape
    rows = 256
    per_b = m // batch // rows
    has_y = y is not None
    modulated = shift is not None
    row_spec = pl.BlockSpec((rows, d), lambda b, i: (b * per_b + i, 0))
    vec_spec = pl.BlockSpec((1, 1, d), lambda b, i: (b, 0, 0))
    ins, specs = [x], [row_spec]
    if has_y:
        ins += [y, gate]
        specs += [row_spec, vec_spec]
    ins.append(g)
    specs.append(pl.BlockSpec((1, d), lambda b, i: (0, 0)))
    if modulated:
        ins += [shift, scale]
        specs += [vec_spec, vec_spec]
    outs, out_specs = [], []
    if has_y:
        outs.append(jax.ShapeDtypeStruct((m, d), F32))
        out_specs.append(row_spec)
    if modulated:
        outs.append(jax.ShapeDtypeStruct((m, d), BF16))
        out_specs.append(row_spec)
    return pl.pallas_call(
        functools.partial(_norm_kernel, has_y=has_y, y_scale=y_scale, modulated=modulated),
        grid=(batch, per_b),
        in_specs=specs,
        out_specs=out_specs,
        out_shape=outs,
        compiler_params=_params(("arbitrary", "arbitrary"), 48),
        name="resid_norm",
    )(*ins)


def _mm_kernel(a_ref, b_ref, o_ref):
    o_ref[...] = jnp.dot(a_ref[...], b_ref[...],
                         preferred_element_type=F32).astype(o_ref.dtype)


def _matmul(a, b, bm, bn, out_dtype, vmem_mib, name):
    m, k = a.shape
    n = b.shape[1]
    return pl.pallas_call(
        _mm_kernel,
        grid=(m // bm, n // bn),
        in_specs=[pl.BlockSpec((bm, k), lambda i, j: (i, 0)),
                  pl.BlockSpec((k, bn), lambda i, j: (0, j))],
        out_specs=pl.BlockSpec((bm, bn), lambda i, j: (i, j)),
        out_shape=jax.ShapeDtypeStruct((m, n), out_dtype),
        compiler_params=_params(("arbitrary", "arbitrary"), vmem_mib),
        name=name,
    )(a, b)


def _with_stationary_weights(compute, sources, wbuf, wbf_ref, sem, prep=None):
    j = pl.program_id(0)
    n_w = wbf_ref.shape[0]

    def copies(jj, slot):
        return [pltpu.make_async_copy(src, wbuf.at[slot, p], sem.at[slot, p])
                for p, src in enumerate(sources(jj))]

    slot = lax.rem(j, 2)
    first = pl.program_id(1) == 0

    @pl.when(first)
    def _():
        @pl.when(j == 0)
        def _():
            for c in copies(0, 0):
                c.start()

        for c in copies(j, slot):
            c.wait()

        @pl.when(j + 1 < pl.num_programs(0))
        def _():
            for c in copies(j + 1, 1 - slot):
                c.start()

        ws = []
        for p in range(n_w):
            w = wbuf[slot, p]
            ws.append((w if prep is None else prep(w)).astype(BF16))
            wbf_ref[p] = ws[p]
        compute(ws)

    @pl.when(jnp.logical_not(first))
    def _():
        compute([wbf_ref[p] for p in range(n_w)])


def _ws_scratch(n_w, rows, cols):
    return [pltpu.VMEM((2, n_w, rows, cols), F32), pltpu.VMEM((n_w, rows, cols), BF16),
            pltpu.SemaphoreType.DMA((2, n_w))]


def _col_block(w_hbm, block, bn):
    return w_hbm.at[:, pl.ds(pl.multiple_of(block * bn, bn), bn)]


def _mm_ws_kernel(a_ref, w_hbm, o_ref, wbuf, wbf_ref, sem):
    bn = o_ref.shape[1]

    def compute(ws):
        o_ref[...] = jnp.dot(a_ref[...], ws[0], preferred_element_type=F32).astype(o_ref.dtype)

    _with_stationary_weights(compute, lambda jj: [_col_block(w_hbm, jj, bn)], wbuf, wbf_ref, sem)


def _matmul_ws(a, w, bm, bn, out_dtype, vmem_mib, name):
    m, k = a.shape
    n = w.shape[1]
    return pl.pallas_call(
        _mm_ws_kernel,
        grid=(n // bn, m // bm),
        in_specs=[pl.BlockSpec((bm, k), lambda j, i: (i, 0)),
                  pl.BlockSpec(memory_space=pl.ANY)],
        out_specs=pl.BlockSpec((bm, bn), lambda j, i: (i, j)),
        out_shape=jax.ShapeDtypeStruct((m, n), out_dtype),
        scratch_shapes=_ws_scratch(1, k, bn),
        compiler_params=_params(("arbitrary", "arbitrary"), vmem_mib),
        name=name,
    )(a, w)


def _ffn_in_kernel(*refs, with_ada):
    if with_ada:
        (h_ref, w_hbm, wo_ref, cb_ref, wada_ref, bada_ref,
         o_ref, wo_bf_ref, mod_ref, wbuf, wbf_ref, sem) = refs
    else:
        h_ref, w_hbm, wo_ref, o_ref, wo_bf_ref, wbuf, wbf_ref, sem = refs
    bn = o_ref.shape[1]
    nb = pl.num_programs(0)

    def compute(ws):
        wo_bf_ref[...] = wo_ref[...].astype(BF16)
        if with_ada:
            w = wada_ref[...]
            rows = [jnp.sum(w * cb_ref[b], axis=0, keepdims=True) for b in range(cb_ref.shape[0])]
            pad = jnp.zeros((mod_ref.shape[0] - len(rows), w.shape[1]), F32)
            mod_ref[...] = jnp.concatenate(rows + [pad], axis=0) + bada_ref[...]
        h = h_ref[...]
        a = jnp.dot(h, ws[0], preferred_element_type=F32)
        b = jnp.dot(h, ws[1], preferred_element_type=F32)
        o_ref[...] = (a * jax.nn.sigmoid(a) * b).astype(o_ref.dtype)

    _with_stationary_weights(
        compute, lambda jj: [_col_block(w_hbm, jj, bn), _col_block(w_hbm, jj + nb, bn)],
        wbuf, wbf_ref, sem)


def _ffn_in(h, w_in, w_out, ada=None):
    m, k = h.shape
    bm, bn = 1024, 256
    nb, mb = D_FF // bn, m // bm
    wo_rows = D_FF // (nb * mb)
    assert wo_rows * nb * mb == D_FF and w_out.shape[0] == D_FF
    wo_spec = pl.BlockSpec((wo_rows, w_out.shape[1]), lambda j, i: (j * mb + i, 0))
    ins = [h, w_in, w_out]
    in_specs = [pl.BlockSpec((bm, k), lambda j, i: (i, 0)),
                pl.BlockSpec(memory_space=pl.ANY),
                wo_spec]
    out_specs = [pl.BlockSpec((bm, bn), lambda j, i: (i, j)), wo_spec]
    out_shape = [jax.ShapeDtypeStruct((m, D_FF), BF16), jax.ShapeDtypeStruct(w_out.shape, BF16)]
    if ada is not None:
        cb, w_ada, b_ada, col0 = ada
        n_rest = w_ada.shape[1] - col0
        n_blk = n_rest // LANE
        assert n_blk * LANE == n_rest and col0 % LANE == 0 and n_blk <= nb * mb
        blk = lambda j, i: jnp.minimum(j * mb + i, n_blk - 1)
        ins += [cb, w_ada, b_ada]
        in_specs += [pl.BlockSpec(cb.shape, lambda j, i: (0, 0, 0), pipeline_mode=pl.Buffered(1)),
                     pl.BlockSpec((k, LANE), lambda j, i: (0, col0 // LANE + blk(j, i))),
                     pl.BlockSpec((1, LANE), lambda j, i: (0, col0 // LANE + blk(j, i)))]
        out_specs.append(pl.BlockSpec((8, LANE), lambda j, i: (0, blk(j, i))))
        out_shape.append(jax.ShapeDtypeStruct((8, n_rest), F32))
    return pl.pallas_call(
        functools.partial(_ffn_in_kernel, with_ada=ada is not None),
        grid=(nb, mb),
        in_specs=in_specs,
        out_specs=out_specs,
        out_shape=out_shape,
        scratch_shapes=_ws_scratch(2, k, bn),
        compiler_params=_params(("arbitrary", "arbitrary"), 56 if ada is not None else 48),
        name="ffn_in_swiglu",
    )(*ins)


def _merge_kernel(oa_ref, ob_ref, wa_hbm, wb_hbm, ga_ref, gb_ref, o_ref, wbuf, wbf_ref, sem):
    bn = o_ref.shape[1]

    def compute(ws):
        ya = jnp.dot(oa_ref[...], ws[0], preferred_element_type=F32)
        yb = jnp.dot(ob_ref[...], ws[1], preferred_element_type=F32)
        ga = jax.nn.sigmoid(ga_ref[...].astype(F32))
        gb = jax.nn.sigmoid(gb_ref[...].astype(F32))
        o_ref[...] = (ga * ya + gb * yb).astype(o_ref.dtype)

    _with_stationary_weights(
        compute, lambda jj: [_col_block(wa_hbm, jj, bn), _col_block(wb_hbm, jj, bn)],
        wbuf, wbf_ref, sem)


def _merge(o_a, o_b, w_up_a, w_up_b, proj):
    m, k = o_a.shape
    bm, bn = 1024, 512
    gb_off = OFF_GATE_B // bn
    return pl.pallas_call(
        _merge_kernel,
        grid=(D_MODEL // bn, m // bm),
        in_specs=[pl.BlockSpec((bm, k), lambda j, i: (i, 0)),
                  pl.BlockSpec((bm, k), lambda j, i: (i, 0)),
                  pl.BlockSpec(memory_space=pl.ANY),
                  pl.BlockSpec(memory_space=pl.ANY),
                  pl.BlockSpec((bm, bn), lambda j, i: (i, j)),
                  pl.BlockSpec((bm, bn), lambda j, i: (i, j + gb_off))],
        out_specs=pl.BlockSpec((bm, bn), lambda j, i: (i, j)),
        out_shape=jax.ShapeDtypeStruct((m, D_MODEL), BF16),
        scratch_shapes=_ws_scratch(2, k, bn),
        compiler_params=_params(("arbitrary", "arbitrary"), 48),
        name="gated_merge",
    )(o_a, o_b, w_up_a, w_up_b, proj, proj)


def _t5_bucket_np(dist):
    n = np.maximum(dist, 0)
    max_exact = N_BUCKETS // 2
    nf = np.maximum(n, max_exact).astype(np.float32)
    large = max_exact + (np.log(nf / max_exact) / math.log(MAX_DISTANCE / max_exact)
                         * (N_BUCKETS - max_exact)).astype(np.int32)
    large = np.minimum(large, N_BUCKETS - 1)
    return np.where(n < max_exact, n, large).astype(np.int32)


def _bias_tile_kernel(rb_ref, bk_ref, o_ref):
    h = pl.program_id(0)
    bk = bk_ref[...]
    far = rb_ref[N_BUCKETS - 1, h]
    acc = jnp.zeros(bk.shape, F32)
    for k in range(N_BUCKETS - 1):
        acc = jnp.where(bk == k, (rb_ref[k, h] - far) * LOG2E, acc)
    o_ref[0] = acc


def _bias_tiles(rel_bias, rows, lead):
    cols = lead + rows
    dist = lead + np.arange(rows)[:, None] - np.arange(cols)[None, :]
    bucket = np.where(dist >= 0, _t5_bucket_np(dist), N_BUCKETS - 1).astype(np.int32)
    assert _t5_bucket_np(np.array([lead + 1]))[0] == N_BUCKETS - 1
    return pl.pallas_call(
        _bias_tile_kernel,
        grid=(N_HEADS,),
        in_specs=[pl.BlockSpec(memory_space=pltpu.SMEM),
                  pl.BlockSpec((rows, cols), lambda h: (0, 0))],
        out_specs=pl.BlockSpec((1, rows, cols), lambda h: (h, 0, 0)),
        out_shape=jax.ShapeDtypeStruct((N_HEADS, rows, cols), F32),
        compiler_params=_params(("arbitrary",), 16),
        name="t5_bias_tiles",
    )(rel_bias, jnp.asarray(bucket))


def _forget_kernel(f_ref, b_ref, o_ref):
    z = f_ref[...] + b_ref[...]
    ls = jnp.minimum(z, 0.0) - jnp.log(1.0 + jnp.exp(-jnp.abs(z)))
    x = ls.T[0:N_HEADS, :]
    seq = x.shape[1]
    lane = lax.broadcasted_iota(jnp.int32, x.shape, 1)
    sh = 1
    while sh < seq:
        x = x + jnp.where(lane >= sh, pltpu.roll(x, sh, 1), 0.0)
        sh *= 2
    o_ref[0] = x * LOG2E


def _forget_cumsum(proj_small, b_forget_pad, batch, seq):
    return pl.pallas_call(
        _forget_kernel,
        grid=(batch,),
        in_specs=[pl.BlockSpec((seq, LANE), lambda b: (b, 3)),
                  pl.BlockSpec((1, LANE), lambda b: (0, 0))],
        out_specs=pl.BlockSpec((1, N_HEADS, seq), lambda b: (b, 0, 0)),
        out_shape=jax.ShapeDtypeStruct((batch, N_HEADS, seq), F32),
        compiler_params=_params(("arbitrary",), 32),
        name="forget_cumsum",
    )(proj_small, b_forget_pad)


_NT = (((1,), (1,)), ((), ()))


def _fox_kernel(q_ref, k_ref, v_ref, f_ref, o_ref, *, tq):
    h = pl.program_id(1)
    seq = q_ref.shape[0]
    frow = f_ref[0, pl.ds(h, 1), :]
    row = lax.broadcasted_iota(jnp.int32, (tq, tq), 0)
    col = lax.broadcasted_iota(jnp.int32, (tq, tq), 1)
    diag_mask = jnp.where(col <= row, 0.0, NEG_BIG)
    for i in range(seq // tq):
        lo, hi = i * tq, (i + 1) * tq
        q = q_ref[lo:hi, :]
        lg_d = (lax.dot_general(q, k_ref[lo:hi, :], _NT, preferred_element_type=F32)
                - frow[:, lo:hi] + diag_mask)
        m = jnp.max(lg_d, axis=-1, keepdims=True)
        if i > 0:
            lg_p = (lax.dot_general(q, k_ref[0:lo, :], _NT, preferred_element_type=F32)
                    - frow[:, 0:lo])
            m = jnp.maximum(m, jnp.max(lg_p, axis=-1, keepdims=True))
        p_d = jnp.exp2(lg_d - m)
        l = jnp.sum(p_d, axis=-1, keepdims=True)
        o = jnp.dot(p_d.astype(BF16), v_ref[lo:hi, :], preferred_element_type=F32)
        if i > 0:
            p_p = jnp.exp2(lg_p - m)
            l = l + jnp.sum(p_p, axis=-1, keepdims=True)
            o = o + jnp.dot(p_p.astype(BF16), v_ref[0:lo, :], preferred_element_type=F32)
        o_ref[lo:hi, :] = (o / l).astype(o_ref.dtype)


def _fox_attention(proj, f_cum, batch, seq):
    tq = 256
    qc, kc, vc = OFF_QB // HEAD_DIM, OFF_KB // HEAD_DIM, OFF_VB // HEAD_DIM
    blk = (seq, HEAD_DIM)
    return pl.pallas_call(
        functools.partial(_fox_kernel, tq=tq),
        grid=(batch, N_HEADS),
        in_specs=[pl.BlockSpec(blk, lambda b, h: (b, qc + h)),
                  pl.BlockSpec(blk, lambda b, h: (b, kc + h)),
                  pl.BlockSpec(blk, lambda b, h: (b, vc + h)),
                  pl.BlockSpec((1, N_HEADS, seq), lambda b, h: (b, 0, 0))],
        out_specs=pl.BlockSpec(blk, lambda b, h: (b, h)),
        out_shape=jax.ShapeDtypeStruct((batch * seq, W_ATT), BF16),
        compiler_params=_params(("arbitrary", "arbitrary"), 32),
        name="fox_attention",
    )(proj, proj, proj, f_cum)


def _dsa_kernel(qi_ref, k0_ref, k1_ref, w_ref, qa_ref, ka_ref, va_ref, dc_ref, o_ref,
                key_ref, mb_ref, lg_ref, *, t_group, sk, topk):
    tq = qa_ref.shape[0]
    t0 = t_group + pl.program_id(1) * tq
    shape = (tq, sk)

    w_lane = lax.broadcasted_iota(jnp.int32, (tq, LANE), 1)
    mb_ref[...] = jnp.zeros(shape, F32)

    def idx_pair(hp, carry):
        q2 = qi_ref[:, pl.ds(pl.multiple_of(hp * LANE, LANE), LANE)]
        w = w_ref[...]
        acc = mb_ref[...]
        for par, k_ref in ((0, k0_ref), (1, k1_ref)):
            s = lax.dot_general(q2, k_ref[0:sk, :].astype(BF16), _NT, preferred_element_type=F32)
            w_h = jnp.sum(jnp.where(w_lane == 2 * hp + par, w, 0.0), axis=-1, keepdims=True)
            acc = acc + jnp.maximum(s, 0.0) * w_h
        mb_ref[...] = acc
        return carry

    lax.fori_loop(0, IDX_HEADS // 2, idx_pair, 0)

    t_pos = lax.broadcasted_iota(jnp.int32, shape, 0) + t0
    s_pos = lax.broadcasted_iota(jnp.int32, shape, 1)
    causal = s_pos <= t_pos
    score = jnp.where(causal, mb_ref[...], -jnp.inf)
    bits = lax.bitcast_convert_type(score, jnp.int32)
    key_ref[...] = jnp.where(bits < 0, bits ^ jnp.int32(0x7FFFFFFF), bits)

    def bit_step(j, thr):
        cand = thr + lax.shift_left(jnp.int32(1), 31 - j)
        cnt = jnp.sum(jnp.where(key_ref[...] >= cand, 1.0, 0.0), axis=-1, keepdims=True)
        return jnp.where(cnt >= topk, cand, thr)

    thr = lax.fori_loop(0, 32, bit_step, jnp.full((tq, 1), INT_MIN, jnp.int32))

    key = key_ref[...]
    ge = key >= thr
    mb_ref[...] = jnp.where(ge & causal, 0.0, NEG_BIG)
    n_ge = jnp.sum(jnp.where(ge, 1.0, 0.0), axis=-1, keepdims=True)
    tied = jnp.where((n_ge > topk) & (thr > KEY_NEG_INF), 1.0, 0.0)

    @pl.when(jnp.max(tied) > 0.0)
    def _():
        kk = key_ref[...]
        gt = kk > thr
        tie = jnp.where(kk == thr, 1.0, 0.0)
        room = topk - jnp.sum(jnp.where(gt, 1.0, 0.0), axis=-1, keepdims=True)
        x = tie
        sh = 1
        while sh < sk:
            x = x + jnp.where(s_pos >= sh, pltpu.roll(x, sh, 1), 0.0)
            sh *= 2
        take = (tie > 0.0) & (x - tie < room)
        mb_ref[...] = jnp.where((gt | take) & causal, 0.0, NEG_BIG)

    unroll = lg_ref.shape[0]
    tile_rows = dc_ref.shape[1]
    lg_ref[:, :, 0:DSA_LEAD] = jnp.zeros((unroll, tq, DSA_LEAD), F32)

    def head_group(g, carry):
        for u in range(unroll):
            h = g * unroll + u
            hcol = pl.ds(pl.multiple_of(h * HEAD_DIM, HEAD_DIM), HEAD_DIM)
            lg_ref[u, :, DSA_LEAD:] = (
                lax.dot_general(qa_ref[:, hcol], ka_ref[0:sk, :], _NT, preferred_element_type=F32)
                + mb_ref[...])
            for r0 in range(0, tq, tile_rows):
                win = pl.ds(pl.multiple_of(t0 + r0, LANE), DSA_LEAD + tile_rows)
                lg_ref[u, r0:r0 + tile_rows, win] += dc_ref[h]
            lg = lg_ref[u, :, DSA_LEAD:]
            p = jnp.exp2(lg - jnp.max(lg, axis=-1, keepdims=True))
            l = jnp.sum(p, axis=-1, keepdims=True)
            o = jnp.dot(p.astype(BF16), va_ref[0:sk, :], preferred_element_type=F32)
            o_ref[0, :, hcol] = (o / l).astype(o_ref.dtype)
        return carry

    lax.fori_loop(0, N_HEADS // unroll, head_group, 0)


def _dsa_attention(proj, proj_small, bias_tiles, batch, seq):
    grp = DSA_GROUP
    topk = min(TOPK_MAX, seq // 4)
    n_idx = IDX_HEADS * IDX_DIM
    tile_rows = bias_tiles.shape[1]
    outs = []
    for g in range(seq // grp):
        sk = (g + 1) * grp
        tq = grp if sk <= 2 * grp else DSA_TQ
        unroll = 8 if sk <= grp else 4
        assert tq % tile_rows == 0
        per_grp = grp // tq
        nq = seq // tq
        qrow = lambda b, j, g=g, nq=nq, per_grp=per_grp: b * nq + g * per_grp + j
        out = pl.pallas_call(
            functools.partial(_dsa_kernel, t_group=g * grp, sk=sk, topk=topk),
            grid=(batch, per_grp),
            in_specs=[pl.BlockSpec((tq, n_idx), lambda b, j, q=qrow: (q(b, j), OFF_QI // n_idx)),
                      pl.BlockSpec((seq, LANE), lambda b, j: (b, 0)),
                      pl.BlockSpec((seq, LANE), lambda b, j: (b, 1)),
                      pl.BlockSpec((tq, LANE), lambda b, j, q=qrow: (q(b, j), 2)),
                      pl.BlockSpec((tq, W_ATT), lambda b, j, q=qrow: (q(b, j), OFF_QA // W_ATT)),
                      pl.BlockSpec((seq, HEAD_DIM), lambda b, j: (b, OFF_KA // HEAD_DIM)),
                      pl.BlockSpec((seq, HEAD_DIM), lambda b, j: (b, OFF_VA // HEAD_DIM)),
                      pl.BlockSpec(bias_tiles.shape, lambda b, j: (0, 0, 0))],
            out_specs=pl.BlockSpec((1, tq, W_ATT), lambda b, j: (b, j, 0)),
            out_shape=jax.ShapeDtypeStruct((batch, grp, W_ATT), BF16),
            scratch_shapes=[pltpu.VMEM((tq, sk), jnp.int32),
                            pltpu.VMEM((tq, sk), F32),
                            pltpu.VMEM((unroll, tq, DSA_LEAD + sk), F32)],
            compiler_params=_params(("arbitrary", "arbitrary"), 48),
            name=f"dsa_attention_k{sk}",
        )(proj, proj_small, proj_small, proj_small, proj, proj, proj, bias_tiles)
        outs.append(out)
    return jnp.concatenate(outs, axis=1).reshape(batch * seq, W_ATT)


_SEG_SIZES = (W_ATT, HEAD_DIM, HEAD_DIM, IDX_HEADS * IDX_DIM, IDX_DIM, IDX_HEADS,
              W_ATT, W_ATT, W_ATT, N_HEADS, D_MODEL, D_MODEL)
(_C_QA, _C_KA, _C_VA, _C_QI, _C_KI, _C_WI, _C_QB, _C_KB, _C_VB, _C_FB, _C_GA, _C_GB) = (
    int(v) for v in np.concatenate([[0], np.cumsum(_SEG_SIZES)[:-1]]))


def _proj_blocks():
    def seg(start, width, is_q=0):
        return [(start + PROJ_BN * t, is_q) for t in range(width // PROJ_BN)]

    blocks = (seg(_C_GA, D_MODEL) + seg(_C_GB, D_MODEL) + seg(_C_QA, W_ATT, 1)
              + seg(_C_QB, W_ATT, 1) + seg(_C_KB, W_ATT) + seg(_C_VB, W_ATT)
              + seg(_C_QI, IDX_HEADS * IDX_DIM))
    assert len(blocks) * PROJ_BN == OFF_KA
    blocks.append((_C_KA, 0))
    assert len(blocks) * PROJ_BN == N_MAIN
    return blocks


def _proj_kernel(src_ref, isq_ref, a_ref, wt_hbm, o_ref, wbuf, wbf_ref, sem):
    bn = o_ref.shape[1]
    scale = jnp.where(isq_ref[pl.program_id(0)] == 1, Q_SCALE, 1.0)

    def compute(ws):
        o_ref[...] = lax.dot_general(a_ref[...], ws[0], _NT,
                                     preferred_element_type=F32).astype(o_ref.dtype)

    _with_stationary_weights(
        compute, lambda jj: [wt_hbm.at[pl.ds(pl.multiple_of(src_ref[jj], 8), bn)]],
        wbuf, wbf_ref, sem, prep=lambda w: w * scale)


def _mixer_proj(h, w_t):
    m, k = h.shape
    bm, bn = 1024, PROJ_BN
    blocks = _proj_blocks()
    src = np.array([b[0] for b in blocks], np.int32)
    assert np.all(src % 8 == 0) and np.all(src + bn <= w_t.shape[0])
    is_q = np.array([b[1] for b in blocks], np.int32)
    nj = len(blocks)
    return pl.pallas_call(
        _proj_kernel,
        grid_spec=pltpu.PrefetchScalarGridSpec(
            num_scalar_prefetch=2,
            grid=(nj, m // bm),
            in_specs=[pl.BlockSpec((bm, k), lambda j, i, *_: (i, 0)),
                      pl.BlockSpec(memory_space=pl.ANY)],
            out_specs=pl.BlockSpec((bm, bn), lambda j, i, *_: (i, j)),
            scratch_shapes=_ws_scratch(1, bn, k)),
        out_shape=jax.ShapeDtypeStruct((m, nj * bn), BF16),
        compiler_params=_params(("arbitrary", "arbitrary"), 48),
        name="mixer_proj",
    )(jnp.asarray(src), jnp.asarray(is_q), h, w_t)


def _small_weight_t(w_t):
    k = w_t.shape[1]
    z = lambda n: jnp.zeros((n, k), w_t.dtype)
    k_i = w_t[_C_KI:_C_KI + IDX_DIM]
    return jnp.concatenate([k_i, z(LANE - IDX_DIM), z(LANE - IDX_DIM), k_i,
                            w_t[_C_WI:_C_WI + IDX_HEADS] * IDX_SCALE, z(LANE - IDX_HEADS),
                            w_t[_C_FB:_C_FB + N_HEADS], z(LANE - N_HEADS)], axis=0).astype(BF16)


def _nt_kernel(a_ref, bt_ref, o_ref):
    o_ref[...] = lax.dot_general(a_ref[...], bt_ref[...], _NT,
                                 preferred_element_type=F32).astype(o_ref.dtype)


def _matmul_nt(a, b_t, bm, out_dtype, vmem_mib, name):
    m, k = a.shape
    n = b_t.shape[0]
    return pl.pallas_call(
        _nt_kernel,
        grid=(m // bm,),
        in_specs=[pl.BlockSpec((bm, k), lambda i: (i, 0)),
                  pl.BlockSpec((n, k), lambda i: (0, 0))],
        out_specs=pl.BlockSpec((bm, n), lambda i: (i, 0)),
        out_shape=jax.ShapeDtypeStruct((m, n), out_dtype),
        compiler_params=_params(("arbitrary",), vmem_mib),
        name=name,
    )(a, b_t)


def _ffn(h, w_in, w_out, ada=None):
    act, w_out_bf, *mod_rest = _ffn_in(h, w_in, w_out, ada)
    return [_matmul(act, w_out_bf, 512, 512, BF16, 56, "ffn_out")] + mod_rest


def kernel(x, c, w_ada, b_ada, g_ffn1, ffn1_w_in, ffn1_w_out, g_mix, w_in, b_forget, rel_bias,
           w_up_a, w_up_b, w_o, g_ffn2, ffn2_w_in, ffn2_w_out, g_final):
    batch, seq, d = x.shape
    assert w_ada.shape[0] == 1, "single-layer trunk"
    xf = x.reshape(batch * seq, d)

    c_lanes = jnp.zeros((LANE, d), F32).at[:batch].set(c)
    bias_tiles = _bias_tiles(rel_bias, DSA_TQ, DSA_LEAD)
    n_first = 2 * d
    mod_first = _ada(c_lanes[:16], w_ada[0], b_ada, n_first)[:batch]
    sh1, sc1 = [v.reshape(batch, 1, d) for v in jnp.split(mod_first, 2, axis=-1)]
    ada_rest = (_ada_bcast(c_lanes, batch), w_ada[0], b_ada, n_first)

    (h,) = _norm(xf, g_ffn1[0][None, :], batch=batch, shift=sh1, scale=sc1)
    y, mod_rest = _ffn(h, ffn1_w_in[0], ffn1_w_out[0], ada_rest)
    gt1, sh2, sc2, gt2, sh3, sc3, gt3 = [
        v.reshape(batch, 1, d) for v in jnp.split(mod_rest[:batch], N_MOD - 2, axis=-1)]

    xf, h = _norm(xf, g_mix[0][None, :], batch=batch, y=y, gate=gt1, y_scale=0.5,
                  shift=sh2, scale=sc2)
    w_t = jnp.swapaxes(w_in, 1, 2)[0]
    proj = _mixer_proj(h, w_t)
    proj_small = _matmul_nt(h, _small_weight_t(w_t), 1024, F32, 40, "mixer_proj_small")
    b_f = jnp.zeros((1, LANE), F32).at[0, :N_HEADS].set(b_forget[0])
    f_cum = _forget_cumsum(proj_small, b_f, batch, seq)
    o_b = _fox_attention(proj, f_cum, batch, seq)
    o_a = _dsa_attention(proj, proj_small, bias_tiles, batch, seq)
    merged = _merge(o_a, o_b, w_up_a[0], w_up_b[0], proj)
    y = _matmul_ws(merged, w_o[0], 1024, 512, BF16, 48, "mixer_out")

    xf, h = _norm(xf, g_ffn2[0][None, :], batch=batch, y=y, gate=gt2, y_scale=1.0,
                  shift=sh3, scale=sc3)
    (y,) = _ffn(h, ffn2_w_in[0], ffn2_w_out[0])
    (out,) = _norm(xf, g_final[None, :], batch=batch, y=y, gate=gt3, y_scale=0.5)
    return out.reshape(batch, seq, d)
```

```python
import functools
import math

import numpy as np
import jax
import jax.numpy as jnp
from jax import lax
from jax.experimental import pallas as pl
from jax.experimental.pallas import tpu as pltpu

F32 = jnp.float32
BF16 = jnp.bfloat16

D_MODEL = 4096
HEAD_DIM = 128
N_HEADS = 16
W_ATT = N_HEADS * HEAD_DIM
IDX_HEADS = 16
IDX_DIM = 64
TOPK_MAX = 256
N_BUCKETS = 32
MAX_DISTANCE = 128
D_FF = 11008
N_MOD = 9
RMS_EPS = 1e-6
LOG2E = math.log2(math.e)
Q_SCALE = HEAD_DIM ** -0.5 * LOG2E
IDX_SCALE = IDX_HEADS ** -0.5 * IDX_DIM ** -0.5

LANE = 128
MIB = 1024 * 1024
NEG_BIG = -1e30
INT_MIN = -(2 ** 31)
KEY_NEG_INF = (0xFF800000 ^ 0x7FFFFFFF) - 2 ** 32

OFF_GATE_A = 0
OFF_GATE_B = OFF_GATE_A + D_MODEL
OFF_QA = OFF_GATE_B + D_MODEL
OFF_QB = OFF_QA + W_ATT
OFF_KB = OFF_QB + W_ATT
OFF_VB = OFF_KB + W_ATT
OFF_QI = OFF_VB + W_ATT
OFF_KA = OFF_QI + IDX_HEADS * IDX_DIM
OFF_VA = OFF_KA + HEAD_DIM
PROJ_BN = 512
N_MAIN = OFF_KA + PROJ_BN
N_SMALL = 4 * LANE

DSA_TQ = 256
DSA_GROUP = 512
DSA_LEAD = LANE


def _params(semantics, vmem_mib):
    return pltpu.CompilerParams(dimension_semantics=semantics,
                                vmem_limit_bytes=vmem_mib * MIB)


def _ada_kernel(c_ref, w_ref, b_ref, o_ref):
    c = c_ref[...]
    ca = (c * jax.nn.sigmoid(c)).astype(BF16)
    o_ref[...] = jnp.dot(ca, w_ref[...].astype(BF16), preferred_element_type=F32) + b_ref[...]


def _ada(c_pad, w, b):
    rows, d = c_pad.shape
    n = w.shape[1]
    bn = 512
    return pl.pallas_call(
        _ada_kernel,
        grid=(n // bn,),
        in_specs=[pl.BlockSpec((rows, d), lambda j: (0, 0)),
                  pl.BlockSpec((d, bn), lambda j: (0, j)),
                  pl.BlockSpec((1, bn), lambda j: (0, j))],
        out_specs=pl.BlockSpec((rows, bn), lambda j: (0, j)),
        out_shape=jax.ShapeDtypeStruct((rows, n), F32),
        compiler_params=_params(("arbitrary",), 40),
        name="ada_mod",
    )(c_pad, w, b)


def _norm_kernel(*refs, has_y, y_scale, modulated):
    it = iter(refs)
    x_ref = next(it)
    if has_y:
        y_ref = next(it)
        gate_ref = next(it)
    g_ref = next(it)
    if modulated:
        shift_ref = next(it)
        scale_ref = next(it)
    x = x_ref[...]
    if has_y:
        x = x + (y_scale * gate_ref[0]) * y_ref[...].astype(F32)
        xo_ref = next(it)
        if modulated:
            xo_ref[...] = x
    ms = jnp.mean(x * x, axis=-1, keepdims=True)
    nrm = x * lax.rsqrt(ms + RMS_EPS) * g_ref[...]
    if modulated:
        h_ref = next(it)
        h_ref[...] = (nrm * (1.0 + scale_ref[0]) + shift_ref[0]).astype(h_ref.dtype)
    else:
        xo_ref[...] = nrm


def _norm(x, g, *, batch, y=None, gate=None, y_scale=1.0, shift=None, scale=None):
    m, d = x.shape
    rows = 256
    per_b = m // batch // rows
    has_y = y is not None
    modulated = shift is not None
    row_spec = pl.BlockSpec((rows, d), lambda b, i: (b * per_b + i, 0))
    vec_spec = pl.BlockSpec((1, 1, d), lambda b, i: (b, 0, 0))
    ins, specs = [x], [row_spec]
    if has_y:
        ins += [y, gate]
        specs += [row_spec, vec_spec]
    ins.append(g)
    specs.append(pl.BlockSpec((1, d), lambda b, i: (0, 0)))
    if modulated:
        ins += [shift, scale]
        specs += [vec_spec, vec_spec]
    outs, out_specs = [], []
    if has_y:
        outs.append(jax.ShapeDtypeStruct((m, d), F32))
        out_specs.append(row_spec)
    if modulated:
        outs.append(jax.ShapeDtypeStruct((m, d), BF16))
        out_specs.append(row_spec)
    return pl.pallas_call(
        functools.partial(_norm_kernel, has_y=has_y, y_scale=y_scale, modulated=modulated),
        grid=(batch, per_b),
        in_specs=specs,
        out_specs=out_specs,
        out_shape=outs,
        compiler_params=_params(("arbitrary", "arbitrary"), 48),
        name="resid_norm",
    )(*ins)


def _mm_kernel(a_ref, b_ref, o_ref):
    o_ref[...] = jnp.dot(a_ref[...], b_ref[...],
                         preferred_element_type=F32).astype(o_ref.dtype)


def _matmul(a, b, bm, bn, out_dtype, vmem_mib, name):
    m, k = a.shape
    n = b.shape[1]
    return pl.pallas_call(
        _mm_kernel,
        grid=(m // bm, n // bn),
        in_specs=[pl.BlockSpec((bm, k), lambda i, j: (i, 0)),
                  pl.BlockSpec((k, bn), lambda i, j: (0, j))],
        out_specs=pl.BlockSpec((bm, bn), lambda i, j: (i, j)),
        out_shape=jax.ShapeDtypeStruct((m, n), out_dtype),
        compiler_params=_params(("arbitrary", "arbitrary"), vmem_mib),
        name=name,
    )(a, b)


def _with_stationary_weights(compute, sources, wbuf, wbf_ref, sem, prep=None):
    j = pl.program_id(0)
    n_w = wbf_ref.shape[0]

    def copies(jj, slot):
        return [pltpu.make_async_copy(src, wbuf.at[slot, p], sem.at[slot, p])
                for p, src in enumerate(sources(jj))]

    slot = lax.rem(j, 2)
    first = pl.program_id(1) == 0

    @pl.when(first)
    def _():
        @pl.when(j == 0)
        def _():
            for c in copies(0, 0):
                c.start()

        for c in copies(j, slot):
            c.wait()

        @pl.when(j + 1 < pl.num_programs(0))
        def _():
            for c in copies(j + 1, 1 - slot):
                c.start()

        ws = []
        for p in range(n_w):
            w = wbuf[slot, p]
            ws.append((w if prep is None else prep(w)).astype(BF16))
            wbf_ref[p] = ws[p]
        compute(ws)

    @pl.when(jnp.logical_not(first))
    def _():
        compute([wbf_ref[p] for p in range(n_w)])


def _ws_scratch(n_w, rows, cols):
    return [pltpu.VMEM((2, n_w, rows, cols), F32), pltpu.VMEM((n_w, rows, cols), BF16),
            pltpu.SemaphoreType.DMA((2, n_w))]


def _col_block(w_hbm, block, bn):
    return w_hbm.at[:, pl.ds(pl.multiple_of(block * bn, bn), bn)]


ACT_SLOTS = 3


def _activation_ring(a_hbms, abuf, asem):
    ni = pl.num_programs(1)
    s = pl.program_id(0) * ni + pl.program_id(1)
    total = pl.num_programs(0) * ni
    bm = abuf.shape[2]

    def copies(step, slot):
        rows = pl.ds(pl.multiple_of(lax.rem(step, ni) * bm, bm), bm)
        return [pltpu.make_async_copy(a.at[rows], abuf.at[slot, p], asem.at[slot, p])
                for p, a in enumerate(a_hbms)]

    @pl.when(s == 0)
    def _():
        for ahead in range(ACT_SLOTS - 1):
            for c in copies(ahead, ahead):
                c.start()

    slot = lax.rem(s, ACT_SLOTS)
    for c in copies(s, slot):
        c.wait()
    nxt = s + (ACT_SLOTS - 1)

    @pl.when(nxt < total)
    def _():
        for c in copies(nxt, lax.rem(nxt, ACT_SLOTS)):
            c.start()

    return [abuf.at[slot, p] for p in range(len(a_hbms))]


def _ring_scratch(n_act, bm, k):
    return [pltpu.VMEM((ACT_SLOTS, n_act, bm, k), BF16), pltpu.SemaphoreType.DMA((ACT_SLOTS, n_act))]


_HBM = pl.BlockSpec(memory_space=pl.ANY)


def _mm_ws_kernel(a_hbm, w_hbm, o_ref, wbuf, wbf_ref, sem, abuf, asem):
    bn = o_ref.shape[1]
    (a_ref,) = _activation_ring([a_hbm], abuf, asem)

    def compute(ws):
        o_ref[...] = jnp.dot(a_ref[...], ws[0], preferred_element_type=F32).astype(o_ref.dtype)

    _with_stationary_weights(compute, lambda jj: [_col_block(w_hbm, jj, bn)], wbuf, wbf_ref, sem)


def _matmul_ws(a, w, bm, bn, out_dtype, vmem_mib, name):
    m, k = a.shape
    n = w.shape[1]
    return pl.pallas_call(
        _mm_ws_kernel,
        grid=(n // bn, m // bm),
        in_specs=[_HBM, _HBM],
        out_specs=pl.BlockSpec((bm, bn), lambda j, i: (i, j)),
        out_shape=jax.ShapeDtypeStruct((m, n), out_dtype),
        scratch_shapes=_ws_scratch(1, k, bn) + _ring_scratch(1, bm, k),
        compiler_params=_params(("arbitrary", "arbitrary"), vmem_mib),
        name=name,
    )(a, w)


def _ffn_in_kernel(h_hbm, w_hbm, wo_ref, o_ref, wo_bf_ref, wbuf, wbf_ref, sem, abuf, asem):
    bn = o_ref.shape[1]
    nb = pl.num_programs(0)
    (h_ref,) = _activation_ring([h_hbm], abuf, asem)

    def compute(ws):
        wo_bf_ref[...] = wo_ref[...].astype(BF16)
        h = h_ref[...]
        a = jnp.dot(h, ws[0], preferred_element_type=F32)
        b = jnp.dot(h, ws[1], preferred_element_type=F32)
        o_ref[...] = (a * jax.nn.sigmoid(a) * b).astype(o_ref.dtype)

    _with_stationary_weights(
        compute, lambda jj: [_col_block(w_hbm, jj, bn), _col_block(w_hbm, jj + nb, bn)],
        wbuf, wbf_ref, sem)


def _ffn_in(h, w_in, w_out):
    m, k = h.shape
    bm, bn = 1024, 256
    nb, mb = D_FF // bn, m // bm
    wo_rows = D_FF // (nb * mb)
    assert wo_rows * nb * mb == D_FF and w_out.shape[0] == D_FF
    wo_spec = pl.BlockSpec((wo_rows, w_out.shape[1]), lambda j, i: (j * mb + i, 0))
    return pl.pallas_call(
        _ffn_in_kernel,
        grid=(nb, mb),
        in_specs=[_HBM, _HBM, wo_spec],
        out_specs=[pl.BlockSpec((bm, bn), lambda j, i: (i, j)), wo_spec],
        out_shape=[jax.ShapeDtypeStruct((m, D_FF), BF16),
                   jax.ShapeDtypeStruct(w_out.shape, BF16)],
        scratch_shapes=_ws_scratch(2, k, bn) + _ring_scratch(1, bm, k),
        compiler_params=_params(("arbitrary", "arbitrary"), 56),
        name="ffn_in_swiglu",
    )(h, w_in, w_out)


def _merge_kernel(oa_hbm, ob_hbm, wa_hbm, wb_hbm, ga_ref, gb_ref, o_ref,
                  wbuf, wbf_ref, sem, abuf, asem):
    bn = o_ref.shape[1]
    oa_ref, ob_ref = _activation_ring([oa_hbm, ob_hbm], abuf, asem)

    def compute(ws):
        ya = jnp.dot(oa_ref[...], ws[0], preferred_element_type=F32)
        yb = jnp.dot(ob_ref[...], ws[1], preferred_element_type=F32)
        ga = jax.nn.sigmoid(ga_ref[...].astype(F32))
        gb = jax.nn.sigmoid(gb_ref[...].astype(F32))
        o_ref[...] = (ga * ya + gb * yb).astype(o_ref.dtype)

    _with_stationary_weights(
        compute, lambda jj: [_col_block(wa_hbm, jj, bn), _col_block(wb_hbm, jj, bn)],
        wbuf, wbf_ref, sem)


def _merge(o_a, o_b, w_up_a, w_up_b, proj):
    m, k = o_a.shape
    bm, bn = 1024, 512
    gb_off = OFF_GATE_B // bn
    return pl.pallas_call(
        _merge_kernel,
        grid=(D_MODEL // bn, m // bm),
        in_specs=[_HBM, _HBM, _HBM, _HBM,
                  pl.BlockSpec((bm, bn), lambda j, i: (i, j)),
                  pl.BlockSpec((bm, bn), lambda j, i: (i, j + gb_off))],
        out_specs=pl.BlockSpec((bm, bn), lambda j, i: (i, j)),
        out_shape=jax.ShapeDtypeStruct((m, D_MODEL), BF16),
        scratch_shapes=_ws_scratch(2, k, bn) + _ring_scratch(2, bm, k),
        compiler_params=_params(("arbitrary", "arbitrary"), 56),
        name="gated_merge",
    )(o_a, o_b, w_up_a, w_up_b, proj, proj)


def _t5_bucket_np(dist):
    n = np.maximum(dist, 0)
    max_exact = N_BUCKETS // 2
    nf = np.maximum(n, max_exact).astype(np.float32)
    large = max_exact + (np.log(nf / max_exact) / math.log(MAX_DISTANCE / max_exact)
                         * (N_BUCKETS - max_exact)).astype(np.int32)
    large = np.minimum(large, N_BUCKETS - 1)
    return np.where(n < max_exact, n, large).astype(np.int32)


def _bias_tile_kernel(rb_ref, bk_ref, o_ref):
    h = pl.program_id(0)
    bk = bk_ref[...]
    far = rb_ref[N_BUCKETS - 1, h]
    acc = jnp.zeros(bk.shape, F32)
    for k in range(N_BUCKETS - 1):
        acc = jnp.where(bk == k, (rb_ref[k, h] - far) * LOG2E, acc)
    o_ref[0] = acc


def _bias_tiles(rel_bias, rows, lead):
    cols = lead + rows
    dist = lead + np.arange(rows)[:, None] - np.arange(cols)[None, :]
    bucket = np.where(dist >= 0, _t5_bucket_np(dist), N_BUCKETS - 1).astype(np.int32)
    assert _t5_bucket_np(np.array([lead + 1]))[0] == N_BUCKETS - 1
    return pl.pallas_call(
        _bias_tile_kernel,
        grid=(N_HEADS,),
        in_specs=[pl.BlockSpec(memory_space=pltpu.SMEM),
                  pl.BlockSpec((rows, cols), lambda h: (0, 0))],
        out_specs=pl.BlockSpec((1, rows, cols), lambda h: (h, 0, 0)),
        out_shape=jax.ShapeDtypeStruct((N_HEADS, rows, cols), F32),
        compiler_params=_params(("arbitrary",), 16),
        name="t5_bias_tiles",
    )(rel_bias, jnp.asarray(bucket))


def _forget_kernel(f_ref, b_ref, o_ref):
    z = f_ref[...] + b_ref[...]
    ls = jnp.minimum(z, 0.0) - jnp.log(1.0 + jnp.exp(-jnp.abs(z)))
    x = ls.T[0:N_HEADS, :]
    seq = x.shape[1]
    lane = lax.broadcasted_iota(jnp.int32, x.shape, 1)
    sh = 1
    while sh < seq:
        x = x + jnp.where(lane >= sh, pltpu.roll(x, sh, 1), 0.0)
        sh *= 2
    o_ref[0] = x * LOG2E


def _forget_cumsum(proj_small, b_forget_pad, batch, seq):
    return pl.pallas_call(
        _forget_kernel,
        grid=(batch,),
        in_specs=[pl.BlockSpec((seq, LANE), lambda b: (b, 3)),
                  pl.BlockSpec((1, LANE), lambda b: (0, 0))],
        out_specs=pl.BlockSpec((1, N_HEADS, seq), lambda b: (b, 0, 0)),
        out_shape=jax.ShapeDtypeStruct((batch, N_HEADS, seq), F32),
        compiler_params=_params(("arbitrary",), 32),
        name="forget_cumsum",
    )(proj_small, b_forget_pad)


_NT = (((1,), (1,)), ((), ()))


def _fox_kernel(q_ref, k_ref, v_ref, f_ref, o_ref, *, tq):
    h = pl.program_id(1)
    seq = q_ref.shape[0]
    frow = f_ref[0, pl.ds(h, 1), :]
    row = lax.broadcasted_iota(jnp.int32, (tq, tq), 0)
    col = lax.broadcasted_iota(jnp.int32, (tq, tq), 1)
    diag_mask = jnp.where(col <= row, 0.0, NEG_BIG)
    for i in range(seq // tq):
        lo, hi = i * tq, (i + 1) * tq
        q = q_ref[lo:hi, :]
        lg_d = (lax.dot_general(q, k_ref[lo:hi, :], _NT, preferred_element_type=F32)
                - frow[:, lo:hi] + diag_mask)
        m = jnp.max(lg_d, axis=-1, keepdims=True)
        if i > 0:
            lg_p = (lax.dot_general(q, k_ref[0:lo, :], _NT, preferred_element_type=F32)
                    - frow[:, 0:lo])
            m = jnp.maximum(m, jnp.max(lg_p, axis=-1, keepdims=True))
        p_d = jnp.exp2(lg_d - m)
        l = jnp.sum(p_d, axis=-1, keepdims=True)
        o = jnp.dot(p_d.astype(BF16), v_ref[lo:hi, :], preferred_element_type=F32)
        if i > 0:
            p_p = jnp.exp2(lg_p - m)
            l = l + jnp.sum(p_p, axis=-1, keepdims=True)
            o = o + jnp.dot(p_p.astype(BF16), v_ref[0:lo, :], preferred_element_type=F32)
        o_ref[lo:hi, :] = (o / l).astype(o_ref.dtype)


def _fox_attention(proj, f_cum, batch, seq):
    tq = 256
    qc, kc, vc = OFF_QB // HEAD_DIM, OFF_KB // HEAD_DIM, OFF_VB // HEAD_DIM
    blk = (seq, HEAD_DIM)
    return pl.pallas_call(
        functools.partial(_fox_kernel, tq=tq),
        grid=(batch, N_HEADS),
        in_specs=[pl.BlockSpec(blk, lambda b, h: (b, qc + h)),
                  pl.BlockSpec(blk, lambda b, h: (b, kc + h)),
                  pl.BlockSpec(blk, lambda b, h: (b, vc + h)),
                  pl.BlockSpec((1, N_HEADS, seq), lambda b, h: (b, 0, 0))],
        out_specs=pl.BlockSpec(blk, lambda b, h: (b, h)),
        out_shape=jax.ShapeDtypeStruct((batch * seq, W_ATT), BF16),
        compiler_params=_params(("arbitrary", "arbitrary"), 32),
        name="fox_attention",
    )(proj, proj, proj, f_cum)


def _dsa_kernel(qi_ref, k0_ref, k1_ref, w_ref, qa_ref, ka_ref, va_ref, dc_ref, o_ref,
                key_ref, mb_ref, lg_ref, *, t_group, sk, topk):
    tq = qa_ref.shape[0]
    t0 = t_group + pl.program_id(1) * tq
    shape = (tq, sk)

    w_lane = lax.broadcasted_iota(jnp.int32, (tq, LANE), 1)
    mb_ref[...] = jnp.zeros(shape, F32)

    def idx_pair(hp, carry):
        q2 = qi_ref[:, pl.ds(pl.multiple_of(hp * LANE, LANE), LANE)]
        w = w_ref[...]
        acc = mb_ref[...]
        for par, k_ref in ((0, k0_ref), (1, k1_ref)):
            s = lax.dot_general(q2, k_ref[0:sk, :].astype(BF16), _NT, preferred_element_type=F32)
            w_h = jnp.sum(jnp.where(w_lane == 2 * hp + par, w, 0.0), axis=-1, keepdims=True)
            acc = acc + jnp.maximum(s, 0.0) * w_h
        mb_ref[...] = acc
        return carry

    lax.fori_loop(0, IDX_HEADS // 2, idx_pair, 0)

    t_pos = lax.broadcasted_iota(jnp.int32, shape, 0) + t0
    s_pos = lax.broadcasted_iota(jnp.int32, shape, 1)
    causal = s_pos <= t_pos
    score = jnp.where(causal, mb_ref[...], -jnp.inf)
    bits = lax.bitcast_convert_type(score, jnp.int32)
    key_ref[...] = jnp.where(bits < 0, bits ^ jnp.int32(0x7FFFFFFF), bits)

    def bit_step(j, thr):
        cand = thr + lax.shift_left(jnp.int32(1), 31 - j)
        cnt = jnp.sum(jnp.where(key_ref[...] >= cand, 1.0, 0.0), axis=-1, keepdims=True)
        return jnp.where(cnt >= topk, cand, thr)

    thr = lax.fori_loop(0, 32, bit_step, jnp.full((tq, 1), INT_MIN, jnp.int32))

    key = key_ref[...]
    ge = key >= thr
    mb_ref[...] = jnp.where(ge & causal, 0.0, NEG_BIG)
    n_ge = jnp.sum(jnp.where(ge, 1.0, 0.0), axis=-1, keepdims=True)
    tied = jnp.where((n_ge > topk) & (thr > KEY_NEG_INF), 1.0, 0.0)

    @pl.when(jnp.max(tied) > 0.0)
    def _():
        kk = key_ref[...]
        gt = kk > thr
        tie = jnp.where(kk == thr, 1.0, 0.0)
        room = topk - jnp.sum(jnp.where(gt, 1.0, 0.0), axis=-1, keepdims=True)
        x = tie
        sh = 1
        while sh < sk:
            x = x + jnp.where(s_pos >= sh, pltpu.roll(x, sh, 1), 0.0)
            sh *= 2
        take = (tie > 0.0) & (x - tie < room)
        mb_ref[...] = jnp.where((gt | take) & causal, 0.0, NEG_BIG)

    unroll = lg_ref.shape[0]
    tile_rows = dc_ref.shape[1]
    lg_ref[:, :, 0:DSA_LEAD] = jnp.zeros((unroll, tq, DSA_LEAD), F32)

    def head_group(g, carry):
        for u in range(unroll):
            h = g * unroll + u
            hcol = pl.ds(pl.multiple_of(h * HEAD_DIM, HEAD_DIM), HEAD_DIM)
            lg_ref[u, :, DSA_LEAD:] = (
                lax.dot_general(qa_ref[:, hcol], ka_ref[0:sk, :], _NT, preferred_element_type=F32)
                + mb_ref[...])
            for r0 in range(0, tq, tile_rows):
                win = pl.ds(pl.multiple_of(t0 + r0, LANE), DSA_LEAD + tile_rows)
                lg_ref[u, r0:r0 + tile_rows, win] += dc_ref[h]
            lg = lg_ref[u, :, DSA_LEAD:]
            p = jnp.exp2(lg - jnp.max(lg, axis=-1, keepdims=True))
            l = jnp.sum(p, axis=-1, keepdims=True)
            o = jnp.dot(p.astype(BF16), va_ref[0:sk, :], preferred_element_type=F32)
            o_ref[0, :, hcol] = (o / l).astype(o_ref.dtype)
        return carry

    lax.fori_loop(0, N_HEADS // unroll, head_group, 0)


def _dsa_attention(proj, proj_small, bias_tiles, batch, seq):
    grp = DSA_GROUP
    topk = min(TOPK_MAX, seq // 4)
    n_idx = IDX_HEADS * IDX_DIM
    tile_rows = bias_tiles.shape[1]
    outs = []
    for g in range(seq // grp):
        sk = (g + 1) * grp
        tq = grp if sk <= 2 * grp else DSA_TQ
        unroll = 8 if sk <= grp else 4
        assert tq % tile_rows == 0
        per_grp = grp // tq
        nq = seq // tq
        qrow = lambda b, j, g=g, nq=nq, per_grp=per_grp: b * nq + g * per_grp + j
        out = pl.pallas_call(
            functools.partial(_dsa_kernel, t_group=g * grp, sk=sk, topk=topk),
            grid=(batch, per_grp),
            in_specs=[pl.BlockSpec((tq, n_idx), lambda b, j, q=qrow: (q(b, j), OFF_QI // n_idx)),
                      pl.BlockSpec((seq, LANE), lambda b, j: (b, 0)),
                      pl.BlockSpec((seq, LANE), lambda b, j: (b, 1)),
                      pl.BlockSpec((tq, LANE), lambda b, j, q=qrow: (q(b, j), 2)),
                      pl.BlockSpec((tq, W_ATT), lambda b, j, q=qrow: (q(b, j), OFF_QA // W_ATT)),
                      pl.BlockSpec((seq, HEAD_DIM), lambda b, j: (b, OFF_KA // HEAD_DIM)),
                      pl.BlockSpec((seq, HEAD_DIM), lambda b, j: (b, OFF_VA // HEAD_DIM)),
                      pl.BlockSpec(bias_tiles.shape, lambda b, j: (0, 0, 0))],
            out_specs=pl.BlockSpec((1, tq, W_ATT), lambda b, j: (b, j, 0)),
            out_shape=jax.ShapeDtypeStruct((batch, grp, W_ATT), BF16),
            scratch_shapes=[pltpu.VMEM((tq, sk), jnp.int32),
                            pltpu.VMEM((tq, sk), F32),
                            pltpu.VMEM((unroll, tq, DSA_LEAD + sk), F32)],
            compiler_params=_params(("arbitrary", "arbitrary"), 48),
            name=f"dsa_attention_k{sk}",
        )(proj, proj_small, proj_small, proj_small, proj, proj, proj, bias_tiles)
        outs.append(out)
    return jnp.concatenate(outs, axis=1).reshape(batch * seq, W_ATT)


_SEG_SIZES = (W_ATT, HEAD_DIM, HEAD_DIM, IDX_HEADS * IDX_DIM, IDX_DIM, IDX_HEADS,
              W_ATT, W_ATT, W_ATT, N_HEADS, D_MODEL, D_MODEL)
(_C_QA, _C_KA, _C_VA, _C_QI, _C_KI, _C_WI, _C_QB, _C_KB, _C_VB, _C_FB, _C_GA, _C_GB) = (
    int(v) for v in np.concatenate([[0], np.cumsum(_SEG_SIZES)[:-1]]))


def _proj_blocks():
    def seg(start, width, is_q=0):
        return [(start + PROJ_BN * t, is_q) for t in range(width // PROJ_BN)]

    blocks = (seg(_C_GA, D_MODEL) + seg(_C_GB, D_MODEL) + seg(_C_QA, W_ATT, 1)
              + seg(_C_QB, W_ATT, 1) + seg(_C_KB, W_ATT) + seg(_C_VB, W_ATT)
              + seg(_C_QI, IDX_HEADS * IDX_DIM))
    assert len(blocks) * PROJ_BN == OFF_KA
    blocks.append((_C_KA, 0))
    assert len(blocks) * PROJ_BN == N_MAIN
    return blocks


def _proj_kernel(src_ref, isq_ref, a_hbm, wt_hbm, o_ref, wbuf, wbf_ref, sem, abuf, asem):
    bn = o_ref.shape[1]
    scale = jnp.where(isq_ref[pl.program_id(0)] == 1, Q_SCALE, 1.0)
    (a_ref,) = _activation_ring([a_hbm], abuf, asem)

    def compute(ws):
        o_ref[...] = lax.dot_general(a_ref[...], ws[0], _NT,
                                     preferred_element_type=F32).astype(o_ref.dtype)

    _with_stationary_weights(
        compute, lambda jj: [wt_hbm.at[pl.ds(pl.multiple_of(src_ref[jj], 8), bn)]],
        wbuf, wbf_ref, sem, prep=lambda w: w * scale)


def _mixer_proj(h, w_t):
    m, k = h.shape
    bm, bn = 1024, PROJ_BN
    blocks = _proj_blocks()
    src = np.array([b[0] for b in blocks], np.int32)
    assert np.all(src % 8 == 0) and np.all(src + bn <= w_t.shape[0])
    is_q = np.array([b[1] for b in blocks], np.int32)
    nj = len(blocks)
    return pl.pallas_call(
        _proj_kernel,
        grid_spec=pltpu.PrefetchScalarGridSpec(
            num_scalar_prefetch=2,
            grid=(nj, m // bm),
            in_specs=[_HBM, _HBM],
            out_specs=pl.BlockSpec((bm, bn), lambda j, i, *_: (i, j)),
            scratch_shapes=_ws_scratch(1, bn, k) + _ring_scratch(1, bm, k)),
        out_shape=jax.ShapeDtypeStruct((m, nj * bn), BF16),
        compiler_params=_params(("arbitrary", "arbitrary"), 56),
        name="mixer_proj",
    )(jnp.asarray(src), jnp.asarray(is_q), h, w_t)


def _small_weight_t(w_t):
    k = w_t.shape[1]
    z = lambda n: jnp.zeros((n, k), w_t.dtype)
    k_i = w_t[_C_KI:_C_KI + IDX_DIM]
    return jnp.concatenate([k_i, z(LANE - IDX_DIM), z(LANE - IDX_DIM), k_i,
                            w_t[_C_WI:_C_WI + IDX_HEADS] * IDX_SCALE, z(LANE - IDX_HEADS),
                            w_t[_C_FB:_C_FB + N_HEADS], z(LANE - N_HEADS)], axis=0).astype(BF16)


def _nt_kernel(a_ref, bt_ref, o_ref):
    o_ref[...] = lax.dot_general(a_ref[...], bt_ref[...], _NT,
                                 preferred_element_type=F32).astype(o_ref.dtype)


def _matmul_nt(a, b_t, bm, out_dtype, vmem_mib, name):
    m, k = a.shape
    n = b_t.shape[0]
    return pl.pallas_call(
        _nt_kernel,
        grid=(m // bm,),
        in_specs=[pl.BlockSpec((bm, k), lambda i: (i, 0)),
                  pl.BlockSpec((n, k), lambda i: (0, 0))],
        out_specs=pl.BlockSpec((bm, n), lambda i: (i, 0)),
        out_shape=jax.ShapeDtypeStruct((m, n), out_dtype),
        compiler_params=_params(("arbitrary",), vmem_mib),
        name=name,
    )(a, b_t)


def _ffn(h, w_in, w_out):
    act, w_out_bf = _ffn_in(h, w_in, w_out)
    return _matmul(act, w_out_bf, 512, 512, BF16, 56, "ffn_out")


def kernel(x, c, w_ada, b_ada, g_ffn1, ffn1_w_in, ffn1_w_out, g_mix, w_in, b_forget, rel_bias,
           w_up_a, w_up_b, w_o, g_ffn2, ffn2_w_in, ffn2_w_out, g_final):
    batch, seq, d = x.shape
    assert w_ada.shape[0] == 1, "single-layer trunk"
    xf = x.reshape(batch * seq, d)

    c_pad = jnp.zeros((16, d), F32).at[:batch].set(c)
    bias_tiles = _bias_tiles(rel_bias, DSA_TQ, DSA_LEAD)
    mod = _ada(c_pad, w_ada[0], b_ada)[:batch]
    sh1, sc1, gt1, sh2, sc2, gt2, sh3, sc3, gt3 = [
        v.reshape(batch, 1, d) for v in jnp.split(mod, N_MOD, axis=-1)]

    (h,) = _norm(xf, g_ffn1[0][None, :], batch=batch, shift=sh1, scale=sc1)
    y = _ffn(h, ffn1_w_in[0], ffn1_w_out[0])

    xf, h = _norm(xf, g_mix[0][None, :], batch=batch, y=y, gate=gt1, y_scale=0.5,
                  shift=sh2, scale=sc2)
    w_t = jnp.swapaxes(w_in, 1, 2)[0]
    proj = _mixer_proj(h, w_t)
    proj_small = _matmul_nt(h, _small_weight_t(w_t), 1024, F32, 40, "mixer_proj_small")
    b_f = jnp.zeros((1, LANE), F32).at[0, :N_HEADS].set(b_forget[0])
    f_cum = _forget_cumsum(proj_small, b_f, batch, seq)
    o_b = _fox_attention(proj, f_cum, batch, seq)
    o_a = _dsa_attention(proj, proj_small, bias_tiles, batch, seq)
    merged = _merge(o_a, o_b, w_up_a[0], w_up_b[0], proj)
    y = _matmul_ws(merged, w_o[0], 1024, 512, BF16, 56, "mixer_out")

    xf, h = _norm(xf, g_ffn2[0][None, :], batch=batch, y=y, gate=gt2, y_scale=1.0,
                  shift=sh3, scale=sc3)
    y = _ffn(h, ffn2_w_in[0], ffn2_w_out[0])
    (out,) = _norm(xf, g_final[None, :], batch=batch, y=y, gate=gt3, y_scale=0.5)
    return out.reshape(batch, seq, d)
```

```python
import functools
import math

import numpy as np
import jax
import jax.numpy as jnp
from jax import lax
from jax.experimental import pallas as pl
from jax.experimental.pallas import tpu as pltpu

F32 = jnp.float32
BF16 = jnp.bfloat16

D_MODEL = 4096
HEAD_DIM = 128
N_HEADS = 16
W_ATT = N_HEADS * HEAD_DIM
IDX_HEADS = 16
IDX_DIM = 64
TOPK_MAX = 256
N_BUCKETS = 32
MAX_DISTANCE = 128
D_FF = 11008
N_MOD = 9
RMS_EPS = 1e-6
LOG2E = math.log2(math.e)
Q_SCALE = HEAD_DIM ** -0.5 * LOG2E
IDX_SCALE = IDX_HEADS ** -0.5 * IDX_DIM ** -0.5

LANE = 128
MIB = 1024 * 1024
NEG_BIG = -1e30
INT_MIN = -(2 ** 31)
KEY_NEG_INF = (0xFF800000 ^ 0x7FFFFFFF) - 2 ** 32

OFF_GATE_A = 0
OFF_GATE_B = OFF_GATE_A + D_MODEL
OFF_QA = OFF_GATE_B + D_MODEL
OFF_QB = OFF_QA + W_ATT
OFF_KB = OFF_QB + W_ATT
OFF_VB = OFF_KB + W_ATT
OFF_QI = OFF_VB + W_ATT
OFF_KA = OFF_QI + IDX_HEADS * IDX_DIM
OFF_VA = OFF_KA + HEAD_DIM
PROJ_BN = 512
N_MAIN = OFF_KA + PROJ_BN
N_SMALL = 4 * LANE

DSA_TQ = 256
DSA_GROUP = 512
DSA_LEAD = LANE


def _params(semantics, vmem_mib):
    return pltpu.CompilerParams(dimension_semantics=semantics,
                                vmem_limit_bytes=vmem_mib * MIB)


def _ada_kernel(c_ref, w_ref, b_ref, o_ref):
    c = c_ref[...]
    ca = (c * jax.nn.sigmoid(c)).astype(BF16)
    o_ref[...] = jnp.dot(ca, w_ref[...].astype(BF16), preferred_element_type=F32) + b_ref[...]


def _ada(c_pad, w, b):
    rows, d = c_pad.shape
    n = w.shape[1]
    bn = 512
    return pl.pallas_call(
        _ada_kernel,
        grid=(n // bn,),
        in_specs=[pl.BlockSpec((rows, d), lambda j: (0, 0)),
                  pl.BlockSpec((d, bn), lambda j: (0, j)),
                  pl.BlockSpec((1, bn), lambda j: (0, j))],
        out_specs=pl.BlockSpec((rows, bn), lambda j: (0, j)),
        out_shape=jax.ShapeDtypeStruct((rows, n), F32),
        compiler_params=_params(("arbitrary",), 40),
        name="ada_mod",
    )(c_pad, w, b)


def _norm_kernel(*refs, has_y, y_scale, modulated):
    it = iter(refs)
    x_ref = next(it)
    if has_y:
        y_ref = next(it)
        gate_ref = next(it)
    g_ref = next(it)
    if modulated:
        shift_ref = next(it)
        scale_ref = next(it)
    x = x_ref[...]
    if has_y:
        x = x + (y_scale * gate_ref[0]) * y_ref[...].astype(F32)
        xo_ref = next(it)
        if modulated:
            xo_ref[...] = x
    ms = jnp.mean(x * x, axis=-1, keepdims=True)
    nrm = x * lax.rsqrt(ms + RMS_EPS) * g_ref[...]
    if modulated:
        h_ref = next(it)
        h_ref[...] = (nrm * (1.0 + scale_ref[0]) + shift_ref[0]).astype(h_ref.dtype)
    else:
        xo_ref[...] = nrm


def _norm(x, g, *, batch, y=None, gate=None, y_scale=1.0, shift=None, scale=None):
    m, d = x.shape
    rows = 256
    per_b = m // batch // rows
    has_y = y is not None
    modulated = shift is not None
    row_spec = pl.BlockSpec((rows, d), lambda b, i: (b * per_b + i, 0))
    vec_spec = pl.BlockSpec((1, 1, d), lambda b, i: (b, 0, 0))
    ins, specs = [x], [row_spec]
    if has_y:
        ins += [y, gate]
        specs += [row_spec, vec_spec]
    ins.append(g)
    specs.append(pl.BlockSpec((1, d), lambda b, i: (0, 0)))
    if modulated:
        ins += [shift, scale]
        specs += [vec_spec, vec_spec]
    outs, out_specs = [], []
    if has_y:
        outs.append(jax.ShapeDtypeStruct((m, d), F32))
        out_specs.append(row_spec)
    if modulated:
        outs.append(jax.ShapeDtypeStruct((m, d), BF16))
        out_specs.append(row_spec)
    return pl.pallas_call(
        functools.partial(_norm_kernel, has_y=has_y, y_scale=y_scale, modulated=modulated),
        grid=(batch, per_b),
        in_specs=specs,
        out_specs=out_specs,
        out_shape=outs,
        compiler_params=_params(("arbitrary", "arbitrary"), 48),
        name="resid_norm",
    )(*ins)


def _mm_kernel(a_ref, b_ref, o_ref):
    o_ref[...] = jnp.dot(a_ref[...], b_ref[...],
                         preferred_element_type=F32).astype(o_ref.dtype)


def _matmul(a, b, bm, bn, out_dtype, vmem_mib, name):
    m, k = a.shape
    n = b.shape[1]
    return pl.pallas_call(
        _mm_kernel,
        grid=(m // bm, n // bn),
        in_specs=[pl.BlockSpec((bm, k), lambda i, j: (i, 0)),
                  pl.BlockSpec((k, bn), lambda i, j: (0, j))],
        out_specs=pl.BlockSpec((bm, bn), lambda i, j: (i, j)),
        out_shape=jax.ShapeDtypeStruct((m, n), out_dtype),
        compiler_params=_params(("arbitrary", "arbitrary"), vmem_mib),
        name=name,
    )(a, b)


def _with_stationary_weights(compute, sources, wbuf, wbf_ref, sem, prep=None):
    j = pl.program_id(0)
    n_w = wbf_ref.shape[0]

    def copies(jj, slot):
        return [pltpu.make_async_copy(src, wbuf.at[slot, p], sem.at[slot, p])
                for p, src in enumerate(sources(jj))]

    slot = lax.rem(j, 2)
    first = pl.program_id(1) == 0

    @pl.when(first)
    def _():
        @pl.when(j == 0)
        def _():
            for c in copies(0, 0):
                c.start()

        for c in copies(j, slot):
            c.wait()

        @pl.when(j + 1 < pl.num_programs(0))
        def _():
            for c in copies(j + 1, 1 - slot):
                c.start()

        ws = []
        for p in range(n_w):
            w = wbuf[slot, p]
            ws.append((w if prep is None else prep(w)).astype(BF16))
            wbf_ref[p] = ws[p]
        compute(ws)

    @pl.when(jnp.logical_not(first))
    def _():
        compute([wbf_ref[p] for p in range(n_w)])


def _ws_scratch(n_w, rows, cols):
    return [pltpu.VMEM((2, n_w, rows, cols), F32), pltpu.VMEM((n_w, rows, cols), BF16),
            pltpu.SemaphoreType.DMA((2, n_w))]


def _col_block(w_hbm, block, bn):
    return w_hbm.at[:, pl.ds(pl.multiple_of(block * bn, bn), bn)]


_HBM = pl.BlockSpec(memory_space=pl.ANY)


def _mm_ws_kernel(a_ref, w_hbm, o_ref, wbuf, wbf_ref, sem):
    bn = o_ref.shape[1]

    def compute(ws):
        o_ref[...] = jnp.dot(a_ref[...], ws[0], preferred_element_type=F32).astype(o_ref.dtype)

    _with_stationary_weights(compute, lambda jj: [_col_block(w_hbm, jj, bn)], wbuf, wbf_ref, sem)


def _matmul_ws(a, w, bm, bn, out_dtype, vmem_mib, name):
    m, k = a.shape
    n = w.shape[1]
    return pl.pallas_call(
        _mm_ws_kernel,
        grid=(n // bn, m // bm),
        in_specs=[pl.BlockSpec((bm, k), lambda j, i: (i, 0)), _HBM],
        out_specs=pl.BlockSpec((bm, bn), lambda j, i: (i, j)),
        out_shape=jax.ShapeDtypeStruct((m, n), out_dtype),
        scratch_shapes=_ws_scratch(1, k, bn),
        compiler_params=_params(("arbitrary", "arbitrary"), vmem_mib),
        name=name,
    )(a, w)


def _ffn_in_kernel(h_ref, w_hbm, wo_ref, o_ref, wo_bf_ref, wbuf, wbf_ref, sem):
    bn = o_ref.shape[1]
    nb = pl.num_programs(0)

    def compute(ws):
        wo_bf_ref[...] = wo_ref[...].astype(BF16)
        h = h_ref[...]
        a = jnp.dot(h, ws[0], preferred_element_type=F32)
        b = jnp.dot(h, ws[1], preferred_element_type=F32)
        o_ref[...] = (a * jax.nn.sigmoid(a) * b).astype(o_ref.dtype)

    _with_stationary_weights(
        compute, lambda jj: [_col_block(w_hbm, jj, bn), _col_block(w_hbm, jj + nb, bn)],
        wbuf, wbf_ref, sem)


def _ffn_in(h, w_in, w_out):
    m, k = h.shape
    bm, bn = 1024, 256
    nb, mb = D_FF // bn, m // bm
    wo_rows = D_FF // (nb * mb)
    assert wo_rows * nb * mb == D_FF and w_out.shape[0] == D_FF
    wo_spec = pl.BlockSpec((wo_rows, w_out.shape[1]), lambda j, i: (j * mb + i, 0))
    return pl.pallas_call(
        _ffn_in_kernel,
        grid=(nb, mb),
        in_specs=[pl.BlockSpec((bm, k), lambda j, i: (i, 0)), _HBM, wo_spec],
        out_specs=[pl.BlockSpec((bm, bn), lambda j, i: (i, j)), wo_spec],
        out_shape=[jax.ShapeDtypeStruct((m, D_FF), BF16),
                   jax.ShapeDtypeStruct(w_out.shape, BF16)],
        scratch_shapes=_ws_scratch(2, k, bn),
        compiler_params=_params(("arbitrary", "arbitrary"), 48),
        name="ffn_in_swiglu",
    )(h, w_in, w_out)


def _merge_kernel(oa_ref, ob_ref, wa_hbm, wb_hbm, ga_ref, gb_ref, o_ref, wbuf, wbf_ref, sem):
    bn = o_ref.shape[1]

    def compute(ws):
        ga = jax.nn.sigmoid(ga_ref[...].astype(F32))
        gb = jax.nn.sigmoid(gb_ref[...].astype(F32))
        ya = jnp.dot(oa_ref[...], ws[0], preferred_element_type=F32)
        yb = jnp.dot(ob_ref[...], ws[1], preferred_element_type=F32)
        o_ref[...] = (ga * ya + gb * yb).astype(o_ref.dtype)

    _with_stationary_weights(
        compute, lambda jj: [_col_block(wa_hbm, jj, bn), _col_block(wb_hbm, jj, bn)],
        wbuf, wbf_ref, sem)


def _merge(o_a, o_b, w_up_a, w_up_b, proj):
    m, k = o_a.shape
    bm, bn = 1024, 512
    gb_off = OFF_GATE_B // bn
    return pl.pallas_call(
        _merge_kernel,
        grid=(D_MODEL // bn, m // bm),
        in_specs=[pl.BlockSpec((bm, k), lambda j, i: (i, 0)),
                  pl.BlockSpec((bm, k), lambda j, i: (i, 0)),
                  _HBM, _HBM,
                  pl.BlockSpec((bm, bn), lambda j, i: (i, j)),
                  pl.BlockSpec((bm, bn), lambda j, i: (i, j + gb_off))],
        out_specs=pl.BlockSpec((bm, bn), lambda j, i: (i, j)),
        out_shape=jax.ShapeDtypeStruct((m, D_MODEL), BF16),
        scratch_shapes=_ws_scratch(2, k, bn),
        compiler_params=_params(("arbitrary", "arbitrary"), 48),
        name="gated_merge",
    )(o_a, o_b, w_up_a, w_up_b, proj, proj)


def _t5_bucket_np(dist):
    n = np.maximum(dist, 0)
    max_exact = N_BUCKETS // 2
    nf = np.maximum(n, max_exact).astype(np.float32)
    large = max_exact + (np.log(nf / max_exact) / math.log(MAX_DISTANCE / max_exact)
                         * (N_BUCKETS - max_exact)).astype(np.int32)
    large = np.minimum(large, N_BUCKETS - 1)
    return np.where(n < max_exact, n, large).astype(np.int32)


def _bias_tile_kernel(rb_ref, bk_ref, o_ref):
    h = pl.program_id(0)
    bk = bk_ref[...]
    far = rb_ref[N_BUCKETS - 1, h]
    acc = jnp.zeros(bk.shape, F32)
    for k in range(N_BUCKETS - 1):
        acc = jnp.where(bk == k, (rb_ref[k, h] - far) * LOG2E, acc)
    o_ref[0] = acc


def _bias_tiles(rel_bias, rows, lead):
    cols = lead + rows
    dist = lead + np.arange(rows)[:, None] - np.arange(cols)[None, :]
    bucket = np.where(dist >= 0, _t5_bucket_np(dist), N_BUCKETS - 1).astype(np.int32)
    assert _t5_bucket_np(np.array([lead + 1]))[0] == N_BUCKETS - 1
    return pl.pallas_call(
        _bias_tile_kernel,
        grid=(N_HEADS,),
        in_specs=[pl.BlockSpec(memory_space=pltpu.SMEM),
                  pl.BlockSpec((rows, cols), lambda h: (0, 0))],
        out_specs=pl.BlockSpec((1, rows, cols), lambda h: (h, 0, 0)),
        out_shape=jax.ShapeDtypeStruct((N_HEADS, rows, cols), F32),
        compiler_params=_params(("arbitrary",), 16),
        name="t5_bias_tiles",
    )(rel_bias, jnp.asarray(bucket))


def _forget_kernel(f_ref, b_ref, o_ref):
    z = f_ref[...] + b_ref[...]
    ls = jnp.minimum(z, 0.0) - jnp.log(1.0 + jnp.exp(-jnp.abs(z)))
    x = ls.T[0:N_HEADS, :]
    seq = x.shape[1]
    lane = lax.broadcasted_iota(jnp.int32, x.shape, 1)
    sh = 1
    while sh < seq:
        x = x + jnp.where(lane >= sh, pltpu.roll(x, sh, 1), 0.0)
        sh *= 2
    o_ref[0] = x * LOG2E


def _forget_cumsum(proj_small, b_forget_pad, batch, seq):
    return pl.pallas_call(
        _forget_kernel,
        grid=(batch,),
        in_specs=[pl.BlockSpec((seq, LANE), lambda b: (b, 3)),
                  pl.BlockSpec((1, LANE), lambda b: (0, 0))],
        out_specs=pl.BlockSpec((1, N_HEADS, seq), lambda b: (b, 0, 0)),
        out_shape=jax.ShapeDtypeStruct((batch, N_HEADS, seq), F32),
        compiler_params=_params(("arbitrary",), 32),
        name="forget_cumsum",
    )(proj_small, b_forget_pad)


_NT = (((1,), (1,)), ((), ()))


def _fox_kernel(q_ref, k_ref, v_ref, f_ref, o_ref, *, tq):
    h = pl.program_id(1)
    seq = q_ref.shape[0]
    frow = f_ref[0, pl.ds(h, 1), :]
    row = lax.broadcasted_iota(jnp.int32, (tq, tq), 0)
    col = lax.broadcasted_iota(jnp.int32, (tq, tq), 1)
    diag_mask = jnp.where(col <= row, 0.0, NEG_BIG)
    n_blk = seq // tq
    lg_ds, lg_ps = [], [None]
    for i in range(n_blk):
        lo, hi = i * tq, (i + 1) * tq
        q = q_ref[lo:hi, :]
        lg_ds.append(lax.dot_general(q, k_ref[lo:hi, :], _NT, preferred_element_type=F32)
                     - frow[:, lo:hi] + diag_mask)
        if i > 0:
            lg_ps.append(lax.dot_general(q, k_ref[0:lo, :], _NT, preferred_element_type=F32)
                         - frow[:, 0:lo])
    for i in range(n_blk):
        lo, hi = i * tq, (i + 1) * tq
        lg_d = lg_ds[i]
        m = jnp.max(lg_d, axis=-1, keepdims=True)
        if i > 0:
            lg_p = lg_ps[i]
            m = jnp.maximum(m, jnp.max(lg_p, axis=-1, keepdims=True))
        p_d = jnp.exp2(lg_d - m)
        l = jnp.sum(p_d, axis=-1, keepdims=True)
        o = jnp.dot(p_d.astype(BF16), v_ref[lo:hi, :], preferred_element_type=F32)
        if i > 0:
            p_p = jnp.exp2(lg_p - m)
            l = l + jnp.sum(p_p, axis=-1, keepdims=True)
            o = o + jnp.dot(p_p.astype(BF16), v_ref[0:lo, :], preferred_element_type=F32)
        o_ref[lo:hi, :] = (o / l).astype(o_ref.dtype)


def _fox_attention(proj, f_cum, batch, seq):
    tq = 256
    qc, kc, vc = OFF_QB // HEAD_DIM, OFF_KB // HEAD_DIM, OFF_VB // HEAD_DIM
    blk = (seq, HEAD_DIM)
    return pl.pallas_call(
        functools.partial(_fox_kernel, tq=tq),
        grid=(batch, N_HEADS),
        in_specs=[pl.BlockSpec(blk, lambda b, h: (b, qc + h)),
                  pl.BlockSpec(blk, lambda b, h: (b, kc + h)),
                  pl.BlockSpec(blk, lambda b, h: (b, vc + h)),
                  pl.BlockSpec((1, N_HEADS, seq), lambda b, h: (b, 0, 0))],
        out_specs=pl.BlockSpec(blk, lambda b, h: (b, h)),
        out_shape=jax.ShapeDtypeStruct((batch * seq, W_ATT), BF16),
        compiler_params=_params(("arbitrary", "arbitrary"), 32),
        name="fox_attention",
    )(proj, proj, proj, f_cum)


def _dsa_kernel(qi_ref, k0_ref, k1_ref, w_ref, qa_ref, ka_ref, va_ref, dc_ref, o_ref,
                key_ref, mb_ref, lg_ref, *, t_group, sk, topk):
    tq = qa_ref.shape[0]
    t0 = t_group + pl.program_id(1) * tq
    shape = (tq, sk)

    w_lane = lax.broadcasted_iota(jnp.int32, (tq, LANE), 1)
    mb_ref[...] = jnp.zeros(shape, F32)

    def idx_pair(hp, carry):
        q2 = qi_ref[:, pl.ds(pl.multiple_of(hp * LANE, LANE), LANE)]
        w = w_ref[...]
        acc = mb_ref[...]
        for par, k_ref in ((0, k0_ref), (1, k1_ref)):
            s = lax.dot_general(q2, k_ref[0:sk, :].astype(BF16), _NT, preferred_element_type=F32)
            w_h = jnp.sum(jnp.where(w_lane == 2 * hp + par, w, 0.0), axis=-1, keepdims=True)
            acc = acc + jnp.maximum(s, 0.0) * w_h
        mb_ref[...] = acc
        return carry

    lax.fori_loop(0, IDX_HEADS // 2, idx_pair, 0)

    t_pos = lax.broadcasted_iota(jnp.int32, shape, 0) + t0
    s_pos = lax.broadcasted_iota(jnp.int32, shape, 1)
    causal = s_pos <= t_pos
    score = jnp.where(causal, mb_ref[...], -jnp.inf)
    bits = lax.bitcast_convert_type(score, jnp.int32)
    key_ref[...] = jnp.where(bits < 0, bits ^ jnp.int32(0x7FFFFFFF), bits)

    def bit_step(j, thr):
        cand = thr + lax.shift_left(jnp.int32(1), 31 - j)
        cnt = jnp.sum(jnp.where(key_ref[...] >= cand, 1.0, 0.0), axis=-1, keepdims=True)
        return jnp.where(cnt >= topk, cand, thr)

    thr = lax.fori_loop(0, 32, bit_step, jnp.full((tq, 1), INT_MIN, jnp.int32))

    key = key_ref[...]
    ge = key >= thr
    mb_ref[...] = jnp.where(ge & causal, 0.0, NEG_BIG)
    n_ge = jnp.sum(jnp.where(ge, 1.0, 0.0), axis=-1, keepdims=True)
    tied = jnp.where((n_ge > topk) & (thr > KEY_NEG_INF), 1.0, 0.0)

    @pl.when(jnp.max(tied) > 0.0)
    def _():
        kk = key_ref[...]
        gt = kk > thr
        tie = jnp.where(kk == thr, 1.0, 0.0)
        room = topk - jnp.sum(jnp.where(gt, 1.0, 0.0), axis=-1, keepdims=True)
        x = tie
        sh = 1
        while sh < sk:
            x = x + jnp.where(s_pos >= sh, pltpu.roll(x, sh, 1), 0.0)
            sh *= 2
        take = (tie > 0.0) & (x - tie < room)
        mb_ref[...] = jnp.where((gt | take) & causal, 0.0, NEG_BIG)

    unroll = lg_ref.shape[0]
    tile_rows = dc_ref.shape[1]
    lg_ref[:, :, 0:DSA_LEAD] = jnp.zeros((unroll, tq, DSA_LEAD), F32)

    def head_group(g, carry):
        for u in range(unroll):
            h = g * unroll + u
            hcol = pl.ds(pl.multiple_of(h * HEAD_DIM, HEAD_DIM), HEAD_DIM)
            lg_ref[u, :, DSA_LEAD:] = (
                lax.dot_general(qa_ref[:, hcol], ka_ref[0:sk, :], _NT, preferred_element_type=F32)
                + mb_ref[...])
            for r0 in range(0, tq, tile_rows):
                win = pl.ds(pl.multiple_of(t0 + r0, LANE), DSA_LEAD + tile_rows)
                lg_ref[u, r0:r0 + tile_rows, win] += dc_ref[h]
            lg = lg_ref[u, :, DSA_LEAD:]
            p = jnp.exp2(lg - jnp.max(lg, axis=-1, keepdims=True))
            l = jnp.sum(p, axis=-1, keepdims=True)
            o = jnp.dot(p.astype(BF16), va_ref[0:sk, :], preferred_element_type=F32)
            o_ref[0, :, hcol] = (o / l).astype(o_ref.dtype)
        return carry

    lax.fori_loop(0, N_HEADS // unroll, head_group, 0)


def _dsa_attention(proj, proj_small, bias_tiles, batch, seq):
    grp = DSA_GROUP
    topk = min(TOPK_MAX, seq // 4)
    n_idx = IDX_HEADS * IDX_DIM
    tile_rows = bias_tiles.shape[1]
    outs = []
    for g in range(seq // grp):
        sk = (g + 1) * grp
        tq = grp if sk <= 2 * grp else DSA_TQ
        unroll = 8 if sk <= grp else 4
        assert tq % tile_rows == 0
        per_grp = grp // tq
        nq = seq // tq
        qrow = lambda b, j, g=g, nq=nq, per_grp=per_grp: b * nq + g * per_grp + j
        out = pl.pallas_call(
            functools.partial(_dsa_kernel, t_group=g * grp, sk=sk, topk=topk),
            grid=(batch, per_grp),
            in_specs=[pl.BlockSpec((tq, n_idx), lambda b, j, q=qrow: (q(b, j), OFF_QI // n_idx)),
                      pl.BlockSpec((seq, LANE), lambda b, j: (b, 0)),
                      pl.BlockSpec((seq, LANE), lambda b, j: (b, 1)),
                      pl.BlockSpec((tq, LANE), lambda b, j, q=qrow: (q(b, j), 2)),
                      pl.BlockSpec((tq, W_ATT), lambda b, j, q=qrow: (q(b, j), OFF_QA // W_ATT)),
                      pl.BlockSpec((seq, HEAD_DIM), lambda b, j: (b, OFF_KA // HEAD_DIM)),
                      pl.BlockSpec((seq, HEAD_DIM), lambda b, j: (b, OFF_VA // HEAD_DIM)),
                      pl.BlockSpec(bias_tiles.shape, lambda b, j: (0, 0, 0))],
            out_specs=pl.BlockSpec((1, tq, W_ATT), lambda b, j: (b, j, 0)),
            out_shape=jax.ShapeDtypeStruct((batch, grp, W_ATT), BF16),
            scratch_shapes=[pltpu.VMEM((tq, sk), jnp.int32),
                            pltpu.VMEM((tq, sk), F32),
                            pltpu.VMEM((unroll, tq, DSA_LEAD + sk), F32)],
            compiler_params=_params(("arbitrary", "arbitrary"), 48),
            name=f"dsa_attention_k{sk}",
        )(proj, proj_small, proj_small, proj_small, proj, proj, proj, bias_tiles)
        outs.append(out)
    return jnp.concatenate(outs, axis=1).reshape(batch * seq, W_ATT)


_SEG_SIZES = (W_ATT, HEAD_DIM, HEAD_DIM, IDX_HEADS * IDX_DIM, IDX_DIM, IDX_HEADS,
              W_ATT, W_ATT, W_ATT, N_HEADS, D_MODEL, D_MODEL)
(_C_QA, _C_KA, _C_VA, _C_QI, _C_KI, _C_WI, _C_QB, _C_KB, _C_VB, _C_FB, _C_GA, _C_GB) = (
    int(v) for v in np.concatenate([[0], np.cumsum(_SEG_SIZES)[:-1]]))


def _proj_blocks():
    def seg(start, width, is_q=0):
        return [(start + PROJ_BN * t, is_q) for t in range(width // PROJ_BN)]

    blocks = (seg(_C_GA, D_MODEL) + seg(_C_GB, D_MODEL) + seg(_C_QA, W_ATT, 1)
              + seg(_C_QB, W_ATT, 1) + seg(_C_KB, W_ATT) + seg(_C_VB, W_ATT)
              + seg(_C_QI, IDX_HEADS * IDX_DIM))
    assert len(blocks) * PROJ_BN == OFF_KA
    blocks.append((_C_KA, 0))
    assert len(blocks) * PROJ_BN == N_MAIN
    return blocks


def _proj_kernel(src_ref, isq_ref, a_ref, wt_hbm, o_ref, wbuf, wbf_ref, sem):
    bn = o_ref.shape[1]
    scale = jnp.where(isq_ref[pl.program_id(0)] == 1, Q_SCALE, 1.0)

    def compute(ws):
        o_ref[...] = lax.dot_general(a_ref[...], ws[0], _NT,
                                     preferred_element_type=F32).astype(o_ref.dtype)

    _with_stationary_weights(
        compute, lambda jj: [wt_hbm.at[pl.ds(pl.multiple_of(src_ref[jj], 8), bn)]],
        wbuf, wbf_ref, sem, prep=lambda w: w * scale)


def _mixer_proj(h, w_t):
    m, k = h.shape
    bm, bn = 1024, PROJ_BN
    blocks = _proj_blocks()
    src = np.array([b[0] for b in blocks], np.int32)
    assert np.all(src % 8 == 0) and np.all(src + bn <= w_t.shape[0])
    is_q = np.array([b[1] for b in blocks], np.int32)
    nj = len(blocks)
    return pl.pallas_call(
        _proj_kernel,
        grid_spec=pltpu.PrefetchScalarGridSpec(
            num_scalar_prefetch=2,
            grid=(nj, m // bm),
            in_specs=[pl.BlockSpec((bm, k), lambda j, i, *_: (i, 0)), _HBM],
            out_specs=pl.BlockSpec((bm, bn), lambda j, i, *_: (i, j)),
            scratch_shapes=_ws_scratch(1, bn, k)),
        out_shape=jax.ShapeDtypeStruct((m, nj * bn), BF16),
        compiler_params=_params(("arbitrary", "arbitrary"), 48),
        name="mixer_proj",
    )(jnp.asarray(src), jnp.asarray(is_q), h, w_t)


def _small_weight_t(w_t):
    k = w_t.shape[1]
    z = lambda n: jnp.zeros((n, k), w_t.dtype)
    k_i = w_t[_C_KI:_C_KI + IDX_DIM]
    return jnp.concatenate([k_i, z(LANE - IDX_DIM), z(LANE - IDX_DIM), k_i,
                            w_t[_C_WI:_C_WI + IDX_HEADS] * IDX_SCALE, z(LANE - IDX_HEADS),
                            w_t[_C_FB:_C_FB + N_HEADS], z(LANE - N_HEADS)], axis=0).astype(BF16)


def _nt_kernel(a_ref, bt_ref, o_ref):
    o_ref[...] = lax.dot_general(a_ref[...], bt_ref[...], _NT,
                                 preferred_element_type=F32).astype(o_ref.dtype)


def _matmul_nt(a, b_t, bm, out_dtype, vmem_mib, name):
    m, k = a.shape
    n = b_t.shape[0]
    return pl.pallas_call(
        _nt_kernel,
        grid=(m // bm,),
        in_specs=[pl.BlockSpec((bm, k), lambda i: (i, 0)),
                  pl.BlockSpec((n, k), lambda i: (0, 0))],
        out_specs=pl.BlockSpec((bm, n), lambda i: (i, 0)),
        out_shape=jax.ShapeDtypeStruct((m, n), out_dtype),
        compiler_params=_params(("arbitrary",), vmem_mib),
        name=name,
    )(a, b_t)


def _ffn(h, w_in, w_out):
    act, w_out_bf = _ffn_in(h, w_in, w_out)
    return _matmul(act, w_out_bf, 512, 512, BF16, 56, "ffn_out")


def kernel(x, c, w_ada, b_ada, g_ffn1, ffn1_w_in, ffn1_w_out, g_mix, w_in, b_forget, rel_bias,
           w_up_a, w_up_b, w_o, g_ffn2, ffn2_w_in, ffn2_w_out, g_final):
    batch, seq, d = x.shape
    assert w_ada.shape[0] == 1, "single-layer trunk"
    xf = x.reshape(batch * seq, d)

    c_pad = jnp.zeros((16, d), F32).at[:batch].set(c)
    bias_tiles = _bias_tiles(rel_bias, DSA_TQ, DSA_LEAD)
    mod = _ada(c_pad, w_ada[0], b_ada)[:batch]
    sh1, sc1, gt1, sh2, sc2, gt2, sh3, sc3, gt3 = [
        v.reshape(batch, 1, d) for v in jnp.split(mod, N_MOD, axis=-1)]

    (h,) = _norm(xf, g_ffn1[0][None, :], batch=batch, shift=sh1, scale=sc1)
    y = _ffn(h, ffn1_w_in[0], ffn1_w_out[0])

    xf, h = _norm(xf, g_mix[0][None, :], batch=batch, y=y, gate=gt1, y_scale=0.5,
                  shift=sh2, scale=sc2)
    w_t = jnp.swapaxes(w_in, 1, 2)[0]
    proj = _mixer_proj(h, w_t)
    proj_small = _matmul_nt(h, _small_weight_t(w_t), 1024, F32, 40, "mixer_proj_small")
    b_f = jnp.zeros((1, LANE), F32).at[0, :N_HEADS].set(b_forget[0])
    f_cum = _forget_cumsum(proj_small, b_f, batch, seq)
    o_b = _fox_attention(proj, f_cum, batch, seq)
    o_a = _dsa_attention(proj, proj_small, bias_tiles, batch, seq)
    merged = _merge(o_a, o_b, w_up_a[0], w_up_b[0], proj)
    y = _matmul_ws(merged, w_o[0], 1024, 512, BF16, 48, "mixer_out")

    xf, h = _norm(xf, g_ffn2[0][None, :], batch=batch, y=y, gate=gt2, y_scale=1.0,
                  shift=sh3, scale=sc3)
    y = _ffn(h, ffn2_w_in[0], ffn2_w_out[0])
    (out,) = _norm(xf, g_final[None, :], batch=batch, y=y, gate=gt3, y_scale=0.5)
    return out.reshape(batch, seq, d)
```

```python
import functools
import math

import numpy as np
import jax
import jax.numpy as jnp
from jax import lax
from jax.experimental import pallas as pl
from jax.experimental.pallas import tpu as pltpu

F32 = jnp.float32
BF16 = jnp.bfloat16

D_MODEL = 4096
HEAD_DIM = 128
N_HEADS = 16
W_ATT = N_HEADS * HEAD_DIM
IDX_HEADS = 16
IDX_DIM = 64
TOPK_MAX = 256
N_BUCKETS = 32
MAX_DISTANCE = 128
D_FF = 11008
N_MOD = 9
RMS_EPS = 1e-6
LOG2E = math.log2(math.e)
Q_SCALE = HEAD_DIM ** -0.5 * LOG2E
IDX_SCALE = IDX_HEADS ** -0.5 * IDX_DIM ** -0.5

LANE = 128
MIB = 1024 * 1024
NEG_BIG = -1e30
INT_MIN = -(2 ** 31)
KEY_NEG_INF = (0xFF800000 ^ 0x7FFFFFFF) - 2 ** 32

OFF_GATE_A = 0
OFF_GATE_B = OFF_GATE_A + D_MODEL
OFF_QA = OFF_GATE_B + D_MODEL
OFF_QB = OFF_QA + W_ATT
OFF_KB = OFF_QB + W_ATT
OFF_VB = OFF_KB + W_ATT
OFF_QI = OFF_VB + W_ATT
OFF_KA = OFF_QI + IDX_HEADS * IDX_DIM
OFF_VA = OFF_KA + HEAD_DIM
PROJ_BN = 512
N_MAIN = OFF_KA + PROJ_BN
N_SMALL = 4 * LANE

DSA_TQ = 256
DSA_GROUP = 512
DSA_LEAD = LANE


def _params(semantics, vmem_mib):
    return pltpu.CompilerParams(dimension_semantics=semantics,
                                vmem_limit_bytes=vmem_mib * MIB)


def _ada_kernel(c_ref, w_ref, b_ref, o_ref):
    c = c_ref[...]
    ca = (c * jax.nn.sigmoid(c)).astype(BF16)
    o_ref[...] = jnp.dot(ca, w_ref[...].astype(BF16), preferred_element_type=F32) + b_ref[...]


def _ada(c_pad, w, b):
    rows, d = c_pad.shape
    n = w.shape[1]
    bn = 512
    return pl.pallas_call(
        _ada_kernel,
        grid=(n // bn,),
        in_specs=[pl.BlockSpec((rows, d), lambda j: (0, 0)),
                  pl.BlockSpec((d, bn), lambda j: (0, j)),
                  pl.BlockSpec((1, bn), lambda j: (0, j))],
        out_specs=pl.BlockSpec((rows, bn), lambda j: (0, j)),
        out_shape=jax.ShapeDtypeStruct((rows, n), F32),
        compiler_params=_params(("arbitrary",), 40),
        name="ada_mod",
    )(c_pad, w, b)


def _norm_kernel(*refs, has_y, y_scale, modulated):
    it = iter(refs)
    x_ref = next(it)
    if has_y:
        y_ref = next(it)
        gate_ref = next(it)
    g_ref = next(it)
    if modulated:
        shift_ref = next(it)
        scale_ref = next(it)
    x = x_ref[...]
    if has_y:
        x = x + (y_scale * gate_ref[0]) * y_ref[...].astype(F32)
        xo_ref = next(it)
        if modulated:
            xo_ref[...] = x
    ms = jnp.mean(x * x, axis=-1, keepdims=True)
    nrm = x * lax.rsqrt(ms + RMS_EPS) * g_ref[...]
    if modulated:
        h_ref = next(it)
        h_ref[...] = (nrm * (1.0 + scale_ref[0]) + shift_ref[0]).astype(h_ref.dtype)
    else:
        xo_ref[...] = nrm


def _norm(x, g, *, batch, y=None, gate=None, y_scale=1.0, shift=None, scale=None):
    m, d = x.shape
    rows = 256
    per_b = m // batch // rows
    has_y = y is not None
    modulated = shift is not None
    row_spec = pl.BlockSpec((rows, d), lambda b, i: (b * per_b + i, 0))
    vec_spec = pl.BlockSpec((1, 1, d), lambda b, i: (b, 0, 0))
    ins, specs = [x], [row_spec]
    if has_y:
        ins += [y, gate]
        specs += [row_spec, vec_spec]
    ins.append(g)
    specs.append(pl.BlockSpec((1, d), lambda b, i: (0, 0)))
    if modulated:
        ins += [shift, scale]
        specs += [vec_spec, vec_spec]
    outs, out_specs = [], []
    if has_y:
        outs.append(jax.ShapeDtypeStruct((m, d), F32))
        out_specs.append(row_spec)
    if modulated:
        outs.append(jax.ShapeDtypeStruct((m, d), BF16))
        out_specs.append(row_spec)
    return pl.pallas_call(
        functools.partial(_norm_kernel, has_y=has_y, y_scale=y_scale, modulated=modulated),
        grid=(batch, per_b),
        in_specs=specs,
        out_specs=out_specs,
        out_shape=outs,
        compiler_params=_params(("arbitrary", "arbitrary"), 48),
        name="resid_norm",
    )(*ins)


def _mm_kernel(a_ref, b_ref, o_ref):
    o_ref[...] = jnp.dot(a_ref[...], b_ref[...],
                         preferred_element_type=F32).astype(o_ref.dtype)


def _matmul(a, b, bm, bn, out_dtype, vmem_mib, name):
    m, k = a.shape
    n = b.shape[1]
    return pl.pallas_call(
        _mm_kernel,
        grid=(m // bm, n // bn),
        in_specs=[pl.BlockSpec((bm, k), lambda i, j: (i, 0)),
                  pl.BlockSpec((k, bn), lambda i, j: (0, j))],
        out_specs=pl.BlockSpec((bm, bn), lambda i, j: (i, j)),
        out_shape=jax.ShapeDtypeStruct((m, n), out_dtype),
        compiler_params=_params(("arbitrary", "arbitrary"), vmem_mib),
        name=name,
    )(a, b)


def _with_stationary_weights(compute, sources, wbuf, wbf_ref, sem, prep=None):
    j = pl.program_id(0)
    n_w = wbf_ref.shape[0]

    def copies(jj, slot):
        return [pltpu.make_async_copy(src, wbuf.at[slot, p], sem.at[slot, p])
                for p, src in enumerate(sources(jj))]

    slot = lax.rem(j, 2)
    first = pl.program_id(1) == 0

    @pl.when(first)
    def _():
        @pl.when(j == 0)
        def _():
            for c in copies(0, 0):
                c.start()

        for c in copies(j, slot):
            c.wait()

        @pl.when(j + 1 < pl.num_programs(0))
        def _():
            for c in copies(j + 1, 1 - slot):
                c.start()

        ws = []
        for p in range(n_w):
            w = wbuf[slot, p]
            ws.append((w if prep is None else prep(w)).astype(BF16))
            wbf_ref[p] = ws[p]
        compute(ws)

    @pl.when(jnp.logical_not(first))
    def _():
        compute([wbf_ref[p] for p in range(n_w)])


def _ws_scratch(n_w, rows, cols):
    return [pltpu.VMEM((2, n_w, rows, cols), F32), pltpu.VMEM((n_w, rows, cols), BF16),
            pltpu.SemaphoreType.DMA((2, n_w))]


def _col_block(w_hbm, block, bn):
    return w_hbm.at[:, pl.ds(pl.multiple_of(block * bn, bn), bn)]


_HBM = pl.BlockSpec(memory_space=pl.ANY)


def _mm_ws_kernel(a_ref, w_hbm, o_ref, wbuf, wbf_ref, sem):
    bn = o_ref.shape[1]

    def compute(ws):
        o_ref[...] = jnp.dot(a_ref[...], ws[0], preferred_element_type=F32).astype(o_ref.dtype)

    _with_stationary_weights(compute, lambda jj: [_col_block(w_hbm, jj, bn)], wbuf, wbf_ref, sem)


def _matmul_ws(a, w, bm, bn, out_dtype, vmem_mib, name):
    m, k = a.shape
    n = w.shape[1]
    return pl.pallas_call(
        _mm_ws_kernel,
        grid=(n // bn, m // bm),
        in_specs=[pl.BlockSpec((bm, k), lambda j, i: (i, 0)), _HBM],
        out_specs=pl.BlockSpec((bm, bn), lambda j, i: (i, j)),
        out_shape=jax.ShapeDtypeStruct((m, n), out_dtype),
        scratch_shapes=_ws_scratch(1, k, bn),
        compiler_params=_params(("arbitrary", "arbitrary"), vmem_mib),
        name=name,
    )(a, w)


def _ffn_in_kernel(h_ref, w_hbm, wo_ref, o_ref, wo_bf_ref, wbuf, wbf_ref, sem):
    bn = o_ref.shape[1]
    nb = pl.num_programs(0)

    def compute(ws):
        wo_bf_ref[...] = wo_ref[...].astype(BF16)
        h = h_ref[...]
        a = jnp.dot(h, ws[0], preferred_element_type=F32)
        b = jnp.dot(h, ws[1], preferred_element_type=F32)
        o_ref[...] = (a * jax.nn.sigmoid(a) * b).astype(o_ref.dtype)

    _with_stationary_weights(
        compute, lambda jj: [_col_block(w_hbm, jj, bn), _col_block(w_hbm, jj + nb, bn)],
        wbuf, wbf_ref, sem)


def _ffn_in(h, w_in, w_out):
    m, k = h.shape
    bm, bn = 1024, 256
    nb, mb = D_FF // bn, m // bm
    wo_rows = D_FF // (nb * mb)
    assert wo_rows * nb * mb == D_FF and w_out.shape[0] == D_FF
    wo_spec = pl.BlockSpec((wo_rows, w_out.shape[1]), lambda j, i: (j * mb + i, 0))
    return pl.pallas_call(
        _ffn_in_kernel,
        grid=(nb, mb),
        in_specs=[pl.BlockSpec((bm, k), lambda j, i: (i, 0)), _HBM, wo_spec],
        out_specs=[pl.BlockSpec((bm, bn), lambda j, i: (i, j)), wo_spec],
        out_shape=[jax.ShapeDtypeStruct((m, D_FF), BF16),
                   jax.ShapeDtypeStruct(w_out.shape, BF16)],
        scratch_shapes=_ws_scratch(2, k, bn),
        compiler_params=_params(("arbitrary", "arbitrary"), 48),
        name="ffn_in_swiglu",
    )(h, w_in, w_out)


def _merge_kernel(oa_ref, ob_ref, wa_hbm, wb_hbm, ga_ref, gb_ref, o_ref, wbuf, wbf_ref, sem):
    bn = o_ref.shape[1]

    def compute(ws):
        ga = jax.nn.sigmoid(ga_ref[...].astype(F32))
        gb = jax.nn.sigmoid(gb_ref[...].astype(F32))
        ya = jnp.dot(oa_ref[...], ws[0], preferred_element_type=F32)
        yb = jnp.dot(ob_ref[...], ws[1], preferred_element_type=F32)
        o_ref[...] = (ga * ya + gb * yb).astype(o_ref.dtype)

    _with_stationary_weights(
        compute, lambda jj: [_col_block(wa_hbm, jj, bn), _col_block(wb_hbm, jj, bn)],
        wbuf, wbf_ref, sem)


def _merge(o_a, o_b, w_up_a, w_up_b, proj):
    m, k = o_a.shape
    bm, bn = 1024, 512
    gb_off = OFF_GATE_B // bn
    return pl.pallas_call(
        _merge_kernel,
        grid=(D_MODEL // bn, m // bm),
        in_specs=[pl.BlockSpec((bm, k), lambda j, i: (i, 0)),
                  pl.BlockSpec((bm, k), lambda j, i: (i, 0)),
                  _HBM, _HBM,
                  pl.BlockSpec((bm, bn), lambda j, i: (i, j)),
                  pl.BlockSpec((bm, bn), lambda j, i: (i, j + gb_off))],
        out_specs=pl.BlockSpec((bm, bn), lambda j, i: (i, j)),
        out_shape=jax.ShapeDtypeStruct((m, D_MODEL), BF16),
        scratch_shapes=_ws_scratch(2, k, bn),
        compiler_params=_params(("arbitrary", "arbitrary"), 48),
        name="gated_merge",
    )(o_a, o_b, w_up_a, w_up_b, proj, proj)


def _t5_bucket_np(dist):
    n = np.maximum(dist, 0)
    max_exact = N_BUCKETS // 2
    nf = np.maximum(n, max_exact).astype(np.float32)
    large = max_exact + (np.log(nf / max_exact) / math.log(MAX_DISTANCE / max_exact)
                         * (N_BUCKETS - max_exact)).astype(np.int32)
    large = np.minimum(large, N_BUCKETS - 1)
    return np.where(n < max_exact, n, large).astype(np.int32)


def _bias_tile_kernel(rb_ref, bk_ref, o_ref):
    h = pl.program_id(0)
    bk = bk_ref[...]
    far = rb_ref[N_BUCKETS - 1, h]
    acc = jnp.zeros(bk.shape, F32)
    for k in range(N_BUCKETS - 1):
        acc = jnp.where(bk == k, (rb_ref[k, h] - far) * LOG2E, acc)
    o_ref[0] = acc


def _bias_tiles(rel_bias, rows, lead):
    cols = lead + rows
    dist = lead + np.arange(rows)[:, None] - np.arange(cols)[None, :]
    bucket = np.where(dist >= 0, _t5_bucket_np(dist), N_BUCKETS - 1).astype(np.int32)
    assert _t5_bucket_np(np.array([lead + 1]))[0] == N_BUCKETS - 1
    return pl.pallas_call(
        _bias_tile_kernel,
        grid=(N_HEADS,),
        in_specs=[pl.BlockSpec(memory_space=pltpu.SMEM),
                  pl.BlockSpec((rows, cols), lambda h: (0, 0))],
        out_specs=pl.BlockSpec((1, rows, cols), lambda h: (h, 0, 0)),
        out_shape=jax.ShapeDtypeStruct((N_HEADS, rows, cols), F32),
        compiler_params=_params(("arbitrary",), 16),
        name="t5_bias_tiles",
    )(rel_bias, jnp.asarray(bucket))


def _forget_kernel(f_ref, b_ref, o_ref):
    z = f_ref[...] + b_ref[...]
    ls = jnp.minimum(z, 0.0) - jnp.log(1.0 + jnp.exp(-jnp.abs(z)))
    x = ls.T[0:N_HEADS, :]
    seq = x.shape[1]
    lane = lax.broadcasted_iota(jnp.int32, x.shape, 1)
    sh = 1
    while sh < seq:
        x = x + jnp.where(lane >= sh, pltpu.roll(x, sh, 1), 0.0)
        sh *= 2
    o_ref[0] = x * LOG2E


def _forget_cumsum(proj_small, b_forget_pad, batch, seq):
    return pl.pallas_call(
        _forget_kernel,
        grid=(batch,),
        in_specs=[pl.BlockSpec((seq, LANE), lambda b: (b, 3)),
                  pl.BlockSpec((1, LANE), lambda b: (0, 0))],
        out_specs=pl.BlockSpec((1, N_HEADS, seq), lambda b: (b, 0, 0)),
        out_shape=jax.ShapeDtypeStruct((batch, N_HEADS, seq), F32),
        compiler_params=_params(("arbitrary",), 32),
        name="forget_cumsum",
    )(proj_small, b_forget_pad)


_NT = (((1,), (1,)), ((), ()))


def _fox_kernel(q_ref, k_ref, v_ref, f_ref, o_ref, *, tq):
    h = pl.program_id(1)
    seq = q_ref.shape[0]
    frow = f_ref[0, pl.ds(h, 1), :]
    row = lax.broadcasted_iota(jnp.int32, (tq, tq), 0)
    col = lax.broadcasted_iota(jnp.int32, (tq, tq), 1)
    diag_mask = jnp.where(col <= row, 0.0, NEG_BIG)
    n_blk = seq // tq
    lg_ds, lg_ps = [], [None]
    for i in range(n_blk):
        lo, hi = i * tq, (i + 1) * tq
        q = q_ref[lo:hi, :]
        lg_ds.append(lax.dot_general(q, k_ref[lo:hi, :], _NT, preferred_element_type=F32)
                     - frow[:, lo:hi] + diag_mask)
        if i > 0:
            lg_ps.append(lax.dot_general(q, k_ref[0:lo, :], _NT, preferred_element_type=F32)
                         - frow[:, 0:lo])
    for i in range(n_blk):
        lo, hi = i * tq, (i + 1) * tq
        lg_d = lg_ds[i]
        m = jnp.max(lg_d, axis=-1, keepdims=True)
        if i > 0:
            lg_p = lg_ps[i]
            m = jnp.maximum(m, jnp.max(lg_p, axis=-1, keepdims=True))
        p_d = jnp.exp2(lg_d - m)
        l = jnp.sum(p_d, axis=-1, keepdims=True)
        o = jnp.dot(p_d.astype(BF16), v_ref[lo:hi, :], preferred_element_type=F32)
        if i > 0:
            p_p = jnp.exp2(lg_p - m)
            l = l + jnp.sum(p_p, axis=-1, keepdims=True)
            o = o + jnp.dot(p_p.astype(BF16), v_ref[0:lo, :], preferred_element_type=F32)
        o_ref[lo:hi, :] = (o / l).astype(o_ref.dtype)


def _fox_attention(proj, f_cum, batch, seq):
    tq = 256
    qc, kc, vc = OFF_QB // HEAD_DIM, OFF_KB // HEAD_DIM, OFF_VB // HEAD_DIM
    blk = (seq, HEAD_DIM)
    return pl.pallas_call(
        functools.partial(_fox_kernel, tq=tq),
        grid=(batch, N_HEADS),
        in_specs=[pl.BlockSpec(blk, lambda b, h: (b, qc + h)),
                  pl.BlockSpec(blk, lambda b, h: (b, kc + h)),
                  pl.BlockSpec(blk, lambda b, h: (b, vc + h)),
                  pl.BlockSpec((1, N_HEADS, seq), lambda b, h: (b, 0, 0))],
        out_specs=pl.BlockSpec(blk, lambda b, h: (b, h)),
        out_shape=jax.ShapeDtypeStruct((batch * seq, W_ATT), BF16),
        compiler_params=_params(("arbitrary", "arbitrary"), 32),
        name="fox_attention",
    )(proj, proj, proj, f_cum)


def _dsa_kernel(qi_ref, k0_ref, k1_ref, w_ref, qa_ref, ka_ref, va_ref, dc_ref, o_ref,
                key_ref, mb_ref, lg_ref, kb_ref, *, t_group, sk, topk):
    tq = qa_ref.shape[0]
    t0 = t_group + pl.program_id(1) * tq
    shape = (tq, sk)

    w_lane = lax.broadcasted_iota(jnp.int32, (tq, LANE), 1)
    mb_ref[...] = jnp.zeros(shape, F32)

    kb_ref[0] = k0_ref[0:sk, :].astype(BF16)
    kb_ref[1] = k1_ref[0:sk, :].astype(BF16)

    def idx_quad(hq, carry):
        w = w_ref[...]
        scores = []
        for t in range(2):
            q2 = qi_ref[:, pl.ds(pl.multiple_of((2 * hq + t) * LANE, LANE), LANE)]
            for par in range(2):
                scores.append(lax.dot_general(q2, kb_ref[par], _NT, preferred_element_type=F32))
        acc = mb_ref[...]
        for n, s in enumerate(scores):
            w_h = jnp.sum(jnp.where(w_lane == 4 * hq + n, w, 0.0), axis=-1, keepdims=True)
            acc = acc + jnp.maximum(s, 0.0) * w_h
        mb_ref[...] = acc
        return carry

    lax.fori_loop(0, IDX_HEADS // 4, idx_quad, 0)

    t_pos = lax.broadcasted_iota(jnp.int32, shape, 0) + t0
    s_pos = lax.broadcasted_iota(jnp.int32, shape, 1)
    causal = s_pos <= t_pos
    score = jnp.where(causal, mb_ref[...], -jnp.inf)
    bits = lax.bitcast_convert_type(score, jnp.int32)
    key_ref[...] = jnp.where(bits < 0, bits ^ jnp.int32(0x7FFFFFFF), bits)

    def bit_step(j, thr):
        cand = thr + lax.shift_left(jnp.int32(1), 31 - j)
        cnt = jnp.sum(jnp.where(key_ref[...] >= cand, 1.0, 0.0), axis=-1, keepdims=True)
        return jnp.where(cnt >= topk, cand, thr)

    thr = lax.fori_loop(0, 32, bit_step, jnp.full((tq, 1), INT_MIN, jnp.int32))

    key = key_ref[...]
    ge = key >= thr
    mb_ref[...] = jnp.where(ge & causal, 0.0, NEG_BIG)
    n_ge = jnp.sum(jnp.where(ge, 1.0, 0.0), axis=-1, keepdims=True)
    tied = jnp.where((n_ge > topk) & (thr > KEY_NEG_INF), 1.0, 0.0)

    @pl.when(jnp.max(tied) > 0.0)
    def _():
        kk = key_ref[...]
        gt = kk > thr
        tie = jnp.where(kk == thr, 1.0, 0.0)
        room = topk - jnp.sum(jnp.where(gt, 1.0, 0.0), axis=-1, keepdims=True)
        x = tie
        sh = 1
        while sh < sk:
            x = x + jnp.where(s_pos >= sh, pltpu.roll(x, sh, 1), 0.0)
            sh *= 2
        take = (tie > 0.0) & (x - tie < room)
        mb_ref[...] = jnp.where((gt | take) & causal, 0.0, NEG_BIG)

    unroll = lg_ref.shape[0]
    tile_rows = dc_ref.shape[1]
    lg_ref[:, :, 0:DSA_LEAD] = jnp.zeros((unroll, tq, DSA_LEAD), F32)

    def head_group(g, carry):
        for u in range(unroll):
            h = g * unroll + u
            hcol = pl.ds(pl.multiple_of(h * HEAD_DIM, HEAD_DIM), HEAD_DIM)
            lg_ref[u, :, DSA_LEAD:] = (
                lax.dot_general(qa_ref[:, hcol], ka_ref[0:sk, :], _NT, preferred_element_type=F32)
                + mb_ref[...])
            for r0 in range(0, tq, tile_rows):
                win = pl.ds(pl.multiple_of(t0 + r0, LANE), DSA_LEAD + tile_rows)
                lg_ref[u, r0:r0 + tile_rows, win] += dc_ref[h]
            lg = lg_ref[u, :, DSA_LEAD:]
            p = jnp.exp2(lg - jnp.max(lg, axis=-1, keepdims=True))
            l = jnp.sum(p, axis=-1, keepdims=True)
            o = jnp.dot(p.astype(BF16), va_ref[0:sk, :], preferred_element_type=F32)
            o_ref[0, :, hcol] = (o / l).astype(o_ref.dtype)
        return carry

    lax.fori_loop(0, N_HEADS // unroll, head_group, 0)


def _dsa_attention(proj, proj_small, bias_tiles, batch, seq):
    grp = DSA_GROUP
    topk = min(TOPK_MAX, seq // 4)
    n_idx = IDX_HEADS * IDX_DIM
    tile_rows = bias_tiles.shape[1]
    outs = []
    for g in range(seq // grp):
        sk = (g + 1) * grp
        tq = grp if sk <= 2 * grp else DSA_TQ
        unroll = 8 if sk <= grp else 4
        assert tq % tile_rows == 0
        per_grp = grp // tq
        nq = seq // tq
        qrow = lambda b, j, g=g, nq=nq, per_grp=per_grp: b * nq + g * per_grp + j
        out = pl.pallas_call(
            functools.partial(_dsa_kernel, t_group=g * grp, sk=sk, topk=topk),
            grid=(batch, per_grp),
            in_specs=[pl.BlockSpec((tq, n_idx), lambda b, j, q=qrow: (q(b, j), OFF_QI // n_idx)),
                      pl.BlockSpec((seq, LANE), lambda b, j: (b, 0)),
                      pl.BlockSpec((seq, LANE), lambda b, j: (b, 1)),
                      pl.BlockSpec((tq, LANE), lambda b, j, q=qrow: (q(b, j), 2)),
                      pl.BlockSpec((tq, W_ATT), lambda b, j, q=qrow: (q(b, j), OFF_QA // W_ATT)),
                      pl.BlockSpec((seq, HEAD_DIM), lambda b, j: (b, OFF_KA // HEAD_DIM)),
                      pl.BlockSpec((seq, HEAD_DIM), lambda b, j: (b, OFF_VA // HEAD_DIM)),
                      pl.BlockSpec(bias_tiles.shape, lambda b, j: (0, 0, 0))],
            out_specs=pl.BlockSpec((1, tq, W_ATT), lambda b, j: (b, j, 0)),
            out_shape=jax.ShapeDtypeStruct((batch, grp, W_ATT), BF16),
            scratch_shapes=[pltpu.VMEM((tq, sk), jnp.int32),
                            pltpu.VMEM((tq, sk), F32),
                            pltpu.VMEM((unroll, tq, DSA_LEAD + sk), F32),
                            pltpu.VMEM((2, sk, LANE), BF16)],
            compiler_params=_params(("arbitrary", "arbitrary"), 48),
            name=f"dsa_attention_k{sk}",
        )(proj, proj_small, proj_small, proj_small, proj, proj, proj, bias_tiles)
        outs.append(out)
    return jnp.concatenate(outs, axis=1).reshape(batch * seq, W_ATT)


_SEG_SIZES = (W_ATT, HEAD_DIM, HEAD_DIM, IDX_HEADS * IDX_DIM, IDX_DIM, IDX_HEADS,
              W_ATT, W_ATT, W_ATT, N_HEADS, D_MODEL, D_MODEL)
(_C_QA, _C_KA, _C_VA, _C_QI, _C_KI, _C_WI, _C_QB, _C_KB, _C_VB, _C_FB, _C_GA, _C_GB) = (
    int(v) for v in np.concatenate([[0], np.cumsum(_SEG_SIZES)[:-1]]))


def _proj_blocks():
    def seg(start, width, is_q=0):
        return [(start + PROJ_BN * t, is_q) for t in range(width // PROJ_BN)]

    blocks = (seg(_C_GA, D_MODEL) + seg(_C_GB, D_MODEL) + seg(_C_QA, W_ATT, 1)
              + seg(_C_QB, W_ATT, 1) + seg(_C_KB, W_ATT) + seg(_C_VB, W_ATT)
              + seg(_C_QI, IDX_HEADS * IDX_DIM))
    assert len(blocks) * PROJ_BN == OFF_KA
    blocks.append((_C_KA, 0))
    assert len(blocks) * PROJ_BN == N_MAIN
    return blocks


def _proj_kernel(src_ref, isq_ref, a_ref, wt_hbm, o_ref, wbuf, wbf_ref, sem):
    bn = o_ref.shape[1]
    scale = jnp.where(isq_ref[pl.program_id(0)] == 1, Q_SCALE, 1.0)

    def compute(ws):
        o_ref[...] = lax.dot_general(a_ref[...], ws[0], _NT,
                                     preferred_element_type=F32).astype(o_ref.dtype)

    _with_stationary_weights(
        compute, lambda jj: [wt_hbm.at[pl.ds(pl.multiple_of(src_ref[jj], 8), bn)]],
        wbuf, wbf_ref, sem, prep=lambda w: w * scale)


def _mixer_proj(h, w_t):
    m, k = h.shape
    bm, bn = 1024, PROJ_BN
    blocks = _proj_blocks()
    src = np.array([b[0] for b in blocks], np.int32)
    assert np.all(src % 8 == 0) and np.all(src + bn <= w_t.shape[0])
    is_q = np.array([b[1] for b in blocks], np.int32)
    nj = len(blocks)
    return pl.pallas_call(
        _proj_kernel,
        grid_spec=pltpu.PrefetchScalarGridSpec(
            num_scalar_prefetch=2,
            grid=(nj, m // bm),
            in_specs=[pl.BlockSpec((bm, k), lambda j, i, *_: (i, 0)), _HBM],
            out_specs=pl.BlockSpec((bm, bn), lambda j, i, *_: (i, j)),
            scratch_shapes=_ws_scratch(1, bn, k)),
        out_shape=jax.ShapeDtypeStruct((m, nj * bn), BF16),
        compiler_params=_params(("arbitrary", "arbitrary"), 48),
        name="mixer_proj",
    )(jnp.asarray(src), jnp.asarray(is_q), h, w_t)


def _small_weight_t(w_t):
    k = w_t.shape[1]
    z = lambda n: jnp.zeros((n, k), w_t.dtype)
    k_i = w_t[_C_KI:_C_KI + IDX_DIM]
    return jnp.concatenate([k_i, z(LANE - IDX_DIM), z(LANE - IDX_DIM), k_i,
                            w_t[_C_WI:_C_WI + IDX_HEADS] * IDX_SCALE, z(LANE - IDX_HEADS),
                            w_t[_C_FB:_C_FB + N_HEADS], z(LANE - N_HEADS)], axis=0).astype(BF16)


def _nt_kernel(a_ref, bt_ref, o_ref):
    o_ref[...] = lax.dot_general(a_ref[...], bt_ref[...], _NT,
                                 preferred_element_type=F32).astype(o_ref.dtype)


def _matmul_nt(a, b_t, bm, out_dtype, vmem_mib, name):
    m, k = a.shape
    n = b_t.shape[0]
    return pl.pallas_call(
        _nt_kernel,
        grid=(m // bm,),
        in_specs=[pl.BlockSpec((bm, k), lambda i: (i, 0)),
                  pl.BlockSpec((n, k), lambda i: (0, 0))],
        out_specs=pl.BlockSpec((bm, n), lambda i: (i, 0)),
        out_shape=jax.ShapeDtypeStruct((m, n), out_dtype),
        compiler_params=_params(("arbitrary",), vmem_mib),
        name=name,
    )(a, b_t)


def _ffn(h, w_in, w_out):
    act, w_out_bf = _ffn_in(h, w_in, w_out)
    return _matmul(act, w_out_bf, 512, 512, BF16, 56, "ffn_out")


def kernel(x, c, w_ada, b_ada, g_ffn1, ffn1_w_in, ffn1_w_out, g_mix, w_in, b_forget, rel_bias,
           w_up_a, w_up_b, w_o, g_ffn2, ffn2_w_in, ffn2_w_out, g_final):
    batch, seq, d = x.shape
    assert w_ada.shape[0] == 1, "single-layer trunk"
    xf = x.reshape(batch * seq, d)

    c_pad = jnp.zeros((16, d), F32).at[:batch].set(c)
    bias_tiles = _bias_tiles(rel_bias, DSA_TQ, DSA_LEAD)
    mod = _ada(c_pad, w_ada[0], b_ada)[:batch]
    sh1, sc1, gt1, sh2, sc2, gt2, sh3, sc3, gt3 = [
        v.reshape(batch, 1, d) for v in jnp.split(mod, N_MOD, axis=-1)]

    (h,) = _norm(xf, g_ffn1[0][None, :], batch=batch, shift=sh1, scale=sc1)
    y = _ffn(h, ffn1_w_in[0], ffn1_w_out[0])

    xf, h = _norm(xf, g_mix[0][None, :], batch=batch, y=y, gate=gt1, y_scale=0.5,
                  shift=sh2, scale=sc2)
    w_t = jnp.swapaxes(w_in, 1, 2)[0]
    proj = _mixer_proj(h, w_t)
    proj_small = _matmul_nt(h, _small_weight_t(w_t), 1024, F32, 40, "mixer_proj_small")
    b_f = jnp.zeros((1, LANE), F32).at[0, :N_HEADS].set(b_forget[0])
    f_cum = _forget_cumsum(proj_small, b_f, batch, seq)
    o_b = _fox_attention(proj, f_cum, batch, seq)
    o_a = _dsa_attention(proj, proj_small, bias_tiles, batch, seq)
    merged = _merge(o_a, o_b, w_up_a[0], w_up_b[0], proj)
    y = _matmul_ws(merged, w_o[0], 1024, 512, BF16, 48, "mixer_out")

    xf, h = _norm(xf, g_ffn2[0][None, :], batch=batch, y=y, gate=gt2, y_scale=1.0,
                  shift=sh3, scale=sc3)
    y = _ffn(h, ffn2_w_in[0], ffn2_w_out[0])
    (out,) = _norm(xf, g_final[None, :], batch=batch, y=y, gate=gt3, y_scale=0.5)
    return out.reshape(batch, seq, d)
```

```python
import functools
import math

import numpy as np
import jax
import jax.numpy as jnp
from jax import lax
from jax.experimental import pallas as pl
from jax.experimental.pallas import tpu as pltpu

F32 = jnp.float32
BF16 = jnp.bfloat16

D_MODEL = 4096
HEAD_DIM = 128
N_HEADS = 16
W_ATT = N_HEADS * HEAD_DIM
IDX_HEADS = 16
IDX_DIM = 64
TOPK_MAX = 256
N_BUCKETS = 32
MAX_DISTANCE = 128
D_FF = 11008
N_MOD = 9
RMS_EPS = 1e-6
LOG2E = math.log2(math.e)
Q_SCALE = HEAD_DIM ** -0.5 * LOG2E
IDX_SCALE = IDX_HEADS ** -0.5 * IDX_DIM ** -0.5

LANE = 128
MIB = 1024 * 1024
NEG_BIG = -1e30
INT_MIN = -(2 ** 31)
KEY_NEG_INF = (0xFF800000 ^ 0x7FFFFFFF) - 2 ** 32

OFF_GATE_A = 0
OFF_GATE_B = OFF_GATE_A + D_MODEL
OFF_QA = OFF_GATE_B + D_MODEL
OFF_QB = OFF_QA + W_ATT
OFF_KB = OFF_QB + W_ATT
OFF_VB = OFF_KB + W_ATT
OFF_QI = OFF_VB + W_ATT
OFF_KA = OFF_QI + IDX_HEADS * IDX_DIM
OFF_VA = OFF_KA + HEAD_DIM
PROJ_BN = 512
N_MAIN = OFF_KA + PROJ_BN
N_SMALL = 4 * LANE

DSA_TQ = 256
DSA_GROUP = 512
DSA_LEAD = LANE


def _params(semantics, vmem_mib):
    return pltpu.CompilerParams(dimension_semantics=semantics,
                                vmem_limit_bytes=vmem_mib * MIB)


def _ada_kernel(c_ref, w_ref, b_ref, o_ref):
    c = c_ref[...]
    ca = (c * jax.nn.sigmoid(c)).astype(BF16)
    o_ref[...] = jnp.dot(ca, w_ref[...].astype(BF16), preferred_element_type=F32) + b_ref[...]


def _ada(c_pad, w, b):
    rows, d = c_pad.shape
    n = w.shape[1]
    bn = 512
    return pl.pallas_call(
        _ada_kernel,
        grid=(n // bn,),
        in_specs=[pl.BlockSpec((rows, d), lambda j: (0, 0)),
                  pl.BlockSpec((d, bn), lambda j: (0, j)),
                  pl.BlockSpec((1, bn), lambda j: (0, j))],
        out_specs=pl.BlockSpec((rows, bn), lambda j: (0, j)),
        out_shape=jax.ShapeDtypeStruct((rows, n), F32),
        compiler_params=_params(("arbitrary",), 40),
        name="ada_mod",
    )(c_pad, w, b)


def _norm_kernel(*refs, has_y, y_scale, modulated):
    it = iter(refs)
    x_ref = next(it)
    if has_y:
        y_ref = next(it)
        gate_ref = next(it)
    g_ref = next(it)
    if modulated:
        shift_ref = next(it)
        scale_ref = next(it)
    x = x_ref[...]
    if has_y:
        x = x + (y_scale * gate_ref[0]) * y_ref[...].astype(F32)
        xo_ref = next(it)
        if modulated:
            xo_ref[...] = x
    ms = jnp.mean(x * x, axis=-1, keepdims=True)
    nrm = x * lax.rsqrt(ms + RMS_EPS) * g_ref[...]
    if modulated:
        h_ref = next(it)
        h_ref[...] = (nrm * (1.0 + scale_ref[0]) + shift_ref[0]).astype(h_ref.dtype)
    else:
        xo_ref[...] = nrm


def _norm(x, g, *, batch, y=None, gate=None, y_scale=1.0, shift=None, scale=None):
    m, d = x.shape
    has_y = y is not None
    modulated = shift is not None
    rows = 256 if has_y else 512
    per_b = m // batch // rows
    row_spec = pl.BlockSpec((rows, d), lambda b, i: (b * per_b + i, 0))
    vec_spec = pl.BlockSpec((1, 1, d), lambda b, i: (b, 0, 0))
    ins, specs = [x], [row_spec]
    if has_y:
        ins += [y, gate]
        specs += [row_spec, vec_spec]
    ins.append(g)
    specs.append(pl.BlockSpec((1, d), lambda b, i: (0, 0)))
    if modulated:
        ins += [shift, scale]
        specs += [vec_spec, vec_spec]
    outs, out_specs = [], []
    if has_y:
        outs.append(jax.ShapeDtypeStruct((m, d), F32))
        out_specs.append(row_spec)
    if modulated:
        outs.append(jax.ShapeDtypeStruct((m, d), BF16))
        out_specs.append(row_spec)
    return pl.pallas_call(
        functools.partial(_norm_kernel, has_y=has_y, y_scale=y_scale, modulated=modulated),
        grid=(batch, per_b),
        in_specs=specs,
        out_specs=out_specs,
        out_shape=outs,
        compiler_params=_params(("arbitrary", "arbitrary"), 48),
        name="resid_norm",
    )(*ins)


def _mm_kernel(a_ref, b_ref, o_ref):
    o_ref[...] = jnp.dot(a_ref[...], b_ref[...],
                         preferred_element_type=F32).astype(o_ref.dtype)


def _matmul(a, b, bm, bn, out_dtype, vmem_mib, name):
    m, k = a.shape
    n = b.shape[1]
    return pl.pallas_call(
        _mm_kernel,
        grid=(m // bm, n // bn),
        in_specs=[pl.BlockSpec((bm, k), lambda i, j: (i, 0)),
                  pl.BlockSpec((k, bn), lambda i, j: (0, j))],
        out_specs=pl.BlockSpec((bm, bn), lambda i, j: (i, j)),
        out_shape=jax.ShapeDtypeStruct((m, n), out_dtype),
        compiler_params=_params(("arbitrary", "arbitrary"), vmem_mib),
        name=name,
    )(a, b)


def _with_stationary_weights(compute, sources, wbuf, wbf_ref, sem, prep=None):
    j = pl.program_id(0)
    n_w = wbf_ref.shape[0]

    def copies(jj, slot):
        return [pltpu.make_async_copy(src, wbuf.at[slot, p], sem.at[slot, p])
                for p, src in enumerate(sources(jj))]

    slot = lax.rem(j, 2)
    first = pl.program_id(1) == 0

    @pl.when(first)
    def _():
        @pl.when(j == 0)
        def _():
            for c in copies(0, 0):
                c.start()

        for c in copies(j, slot):
            c.wait()

        @pl.when(j + 1 < pl.num_programs(0))
        def _():
            for c in copies(j + 1, 1 - slot):
                c.start()

        ws = []
        for p in range(n_w):
            w = wbuf[slot, p]
            ws.append((w if prep is None else prep(w)).astype(BF16))
            wbf_ref[p] = ws[p]
        compute(ws)

    @pl.when(jnp.logical_not(first))
    def _():
        compute([wbf_ref[p] for p in range(n_w)])


def _ws_scratch(n_w, rows, cols):
    return [pltpu.VMEM((2, n_w, rows, cols), F32), pltpu.VMEM((n_w, rows, cols), BF16),
            pltpu.SemaphoreType.DMA((2, n_w))]


def _col_block(w_hbm, block, bn):
    return w_hbm.at[:, pl.ds(pl.multiple_of(block * bn, bn), bn)]


_HBM = pl.BlockSpec(memory_space=pl.ANY)


def _mm_ws_kernel(a_ref, w_hbm, o_ref, wbuf, wbf_ref, sem):
    bn = o_ref.shape[1]

    def compute(ws):
        o_ref[...] = jnp.dot(a_ref[...], ws[0], preferred_element_type=F32).astype(o_ref.dtype)

    _with_stationary_weights(compute, lambda jj: [_col_block(w_hbm, jj, bn)], wbuf, wbf_ref, sem)


def _matmul_ws(a, w, bm, bn, out_dtype, vmem_mib, name):
    m, k = a.shape
    n = w.shape[1]
    return pl.pallas_call(
        _mm_ws_kernel,
        grid=(n // bn, m // bm),
        in_specs=[pl.BlockSpec((bm, k), lambda j, i: (i, 0)), _HBM],
        out_specs=pl.BlockSpec((bm, bn), lambda j, i: (i, j)),
        out_shape=jax.ShapeDtypeStruct((m, n), out_dtype),
        scratch_shapes=_ws_scratch(1, k, bn),
        compiler_params=_params(("arbitrary", "arbitrary"), vmem_mib),
        name=name,
    )(a, w)


def _ffn_in_kernel(h_ref, w_hbm, wo_ref, o_ref, wo_bf_ref, wbuf, wbf_ref, sem):
    bn = o_ref.shape[1]
    nb = pl.num_programs(0)

    def compute(ws):
        wo_bf_ref[...] = wo_ref[...].astype(BF16)
        h = h_ref[...]
        a = jnp.dot(h, ws[0], preferred_element_type=F32)
        b = jnp.dot(h, ws[1], preferred_element_type=F32)
        o_ref[...] = (a * jax.nn.sigmoid(a) * b).astype(o_ref.dtype)

    _with_stationary_weights(
        compute, lambda jj: [_col_block(w_hbm, jj, bn), _col_block(w_hbm, jj + nb, bn)],
        wbuf, wbf_ref, sem)


def _ffn_in(h, w_in, w_out):
    m, k = h.shape
    bm, bn = 1024, 256
    nb, mb = D_FF // bn, m // bm
    wo_rows = D_FF // (nb * mb)
    assert wo_rows * nb * mb == D_FF and w_out.shape[0] == D_FF
    wo_spec = pl.BlockSpec((wo_rows, w_out.shape[1]), lambda j, i: (j * mb + i, 0))
    return pl.pallas_call(
        _ffn_in_kernel,
        grid=(nb, mb),
        in_specs=[pl.BlockSpec((bm, k), lambda j, i: (i, 0)), _HBM, wo_spec],
        out_specs=[pl.BlockSpec((bm, bn), lambda j, i: (i, j)), wo_spec],
        out_shape=[jax.ShapeDtypeStruct((m, D_FF), BF16),
                   jax.ShapeDtypeStruct(w_out.shape, BF16)],
        scratch_shapes=_ws_scratch(2, k, bn),
        compiler_params=_params(("arbitrary", "arbitrary"), 48),
        name="ffn_in_swiglu",
    )(h, w_in, w_out)


def _merge_kernel(oa_ref, ob_ref, wa_hbm, wb_hbm, ga_ref, gb_ref, o_ref, wbuf, wbf_ref, sem):
    bn = o_ref.shape[1]

    def compute(ws):
        ga = jax.nn.sigmoid(ga_ref[...].astype(F32))
        gb = jax.nn.sigmoid(gb_ref[...].astype(F32))
        ya = jnp.dot(oa_ref[...], ws[0], preferred_element_type=F32)
        yb = jnp.dot(ob_ref[...], ws[1], preferred_element_type=F32)
        o_ref[...] = (ga * ya + gb * yb).astype(o_ref.dtype)

    _with_stationary_weights(
        compute, lambda jj: [_col_block(wa_hbm, jj, bn), _col_block(wb_hbm, jj, bn)],
        wbuf, wbf_ref, sem)


def _merge(o_a, o_b, w_up_a, w_up_b, proj):
    m, k = o_a.shape
    bm, bn = 1024, 512
    gb_off = OFF_GATE_B // bn
    return pl.pallas_call(
        _merge_kernel,
        grid=(D_MODEL // bn, m // bm),
        in_specs=[pl.BlockSpec((bm, k), lambda j, i: (i, 0)),
                  pl.BlockSpec((bm, k), lambda j, i: (i, 0)),
                  _HBM, _HBM,
                  pl.BlockSpec((bm, bn), lambda j, i: (i, j)),
                  pl.BlockSpec((bm, bn), lambda j, i: (i, j + gb_off))],
        out_specs=pl.BlockSpec((bm, bn), lambda j, i: (i, j)),
        out_shape=jax.ShapeDtypeStruct((m, D_MODEL), BF16),
        scratch_shapes=_ws_scratch(2, k, bn),
        compiler_params=_params(("arbitrary", "arbitrary"), 48),
        name="gated_merge",
    )(o_a, o_b, w_up_a, w_up_b, proj, proj)


def _t5_bucket_np(dist):
    n = np.maximum(dist, 0)
    max_exact = N_BUCKETS // 2
    nf = np.maximum(n, max_exact).astype(np.float32)
    large = max_exact + (np.log(nf / max_exact) / math.log(MAX_DISTANCE / max_exact)
                         * (N_BUCKETS - max_exact)).astype(np.int32)
    large = np.minimum(large, N_BUCKETS - 1)
    return np.where(n < max_exact, n, large).astype(np.int32)


def _bias_tile_kernel(rb_ref, bk_ref, o_ref):
    def one_head(h, carry):
        bk = bk_ref[...]
        far = rb_ref[N_BUCKETS - 1, h]
        acc = jnp.zeros(bk.shape, F32)
        for k in range(N_BUCKETS - 1):
            acc = jnp.where(bk == k, (rb_ref[k, h] - far) * LOG2E, acc)
        o_ref[h] = acc
        return carry

    lax.fori_loop(0, o_ref.shape[0], one_head, 0)


def _bias_tiles(rel_bias, rows, lead):
    cols = lead + rows
    dist = lead + np.arange(rows)[:, None] - np.arange(cols)[None, :]
    bucket = np.where(dist >= 0, _t5_bucket_np(dist), N_BUCKETS - 1).astype(np.int32)
    assert _t5_bucket_np(np.array([lead + 1]))[0] == N_BUCKETS - 1
    return pl.pallas_call(
        _bias_tile_kernel,
        grid=(1,),
        in_specs=[pl.BlockSpec(memory_space=pltpu.SMEM),
                  pl.BlockSpec((rows, cols), lambda i: (0, 0))],
        out_specs=pl.BlockSpec((N_HEADS, rows, cols), lambda i: (0, 0, 0)),
        out_shape=jax.ShapeDtypeStruct((N_HEADS, rows, cols), F32),
        compiler_params=_params(("arbitrary",), 32),
        name="t5_bias_tiles",
    )(rel_bias, jnp.asarray(bucket))


def _forget_kernel(f_ref, b_ref, o_ref):
    z = f_ref[...] + b_ref[...]
    ls = jnp.minimum(z, 0.0) - jnp.log(1.0 + jnp.exp(-jnp.abs(z)))
    x = ls.T[0:N_HEADS, :]
    seq = x.shape[1]
    lane = lax.broadcasted_iota(jnp.int32, x.shape, 1)
    sh = 1
    while sh < seq:
        x = x + jnp.where(lane >= sh, pltpu.roll(x, sh, 1), 0.0)
        sh *= 2
    o_ref[0] = x * LOG2E


def _forget_cumsum(proj_small, b_forget_pad, batch, seq):
    return pl.pallas_call(
        _forget_kernel,
        grid=(batch,),
        in_specs=[pl.BlockSpec((seq, LANE), lambda b: (b, 3)),
                  pl.BlockSpec((1, LANE), lambda b: (0, 0))],
        out_specs=pl.BlockSpec((1, N_HEADS, seq), lambda b: (b, 0, 0)),
        out_shape=jax.ShapeDtypeStruct((batch, N_HEADS, seq), F32),
        compiler_params=_params(("arbitrary",), 32),
        name="forget_cumsum",
    )(proj_small, b_forget_pad)


_NT = (((1,), (1,)), ((), ()))


def _fox_kernel(q_ref, k_ref, v_ref, f_ref, o_ref, *, tq):
    h = pl.program_id(1)
    seq = q_ref.shape[0]
    frow = f_ref[0, pl.ds(h, 1), :]
    row = lax.broadcasted_iota(jnp.int32, (tq, tq), 0)
    col = lax.broadcasted_iota(jnp.int32, (tq, tq), 1)
    diag_mask = jnp.where(col <= row, 0.0, NEG_BIG)
    n_blk = seq // tq
    lg_ds, lg_ps = [], [None]
    for i in range(n_blk):
        lo, hi = i * tq, (i + 1) * tq
        q = q_ref[lo:hi, :]
        lg_ds.append(lax.dot_general(q, k_ref[lo:hi, :], _NT, preferred_element_type=F32)
                     - frow[:, lo:hi] + diag_mask)
        if i > 0:
            lg_ps.append(lax.dot_general(q, k_ref[0:lo, :], _NT, preferred_element_type=F32)
                         - frow[:, 0:lo])
    for i in range(n_blk):
        lo, hi = i * tq, (i + 1) * tq
        lg_d = lg_ds[i]
        m = jnp.max(lg_d, axis=-1, keepdims=True)
        if i > 0:
            lg_p = lg_ps[i]
            m = jnp.maximum(m, jnp.max(lg_p, axis=-1, keepdims=True))
        p_d = jnp.exp2(lg_d - m)
        l = jnp.sum(p_d, axis=-1, keepdims=True)
        o = jnp.dot(p_d.astype(BF16), v_ref[lo:hi, :], preferred_element_type=F32)
        if i > 0:
            p_p = jnp.exp2(lg_p - m)
            l = l + jnp.sum(p_p, axis=-1, keepdims=True)
            o = o + jnp.dot(p_p.astype(BF16), v_ref[0:lo, :], preferred_element_type=F32)
        o_ref[lo:hi, :] = (o / l).astype(o_ref.dtype)


def _fox_attention(proj, f_cum, batch, seq):
    tq = 256
    qc, kc, vc = OFF_QB // HEAD_DIM, OFF_KB // HEAD_DIM, OFF_VB // HEAD_DIM
    blk = (seq, HEAD_DIM)
    return pl.pallas_call(
        functools.partial(_fox_kernel, tq=tq),
        grid=(batch, N_HEADS),
        in_specs=[pl.BlockSpec(blk, lambda b, h: (b, qc + h)),
                  pl.BlockSpec(blk, lambda b, h: (b, kc + h)),
                  pl.BlockSpec(blk, lambda b, h: (b, vc + h)),
                  pl.BlockSpec((1, N_HEADS, seq), lambda b, h: (b, 0, 0))],
        out_specs=pl.BlockSpec(blk, lambda b, h: (b, h)),
        out_shape=jax.ShapeDtypeStruct((batch * seq, W_ATT), BF16),
        compiler_params=_params(("arbitrary", "arbitrary"), 32),
        name="fox_attention",
    )(proj, proj, proj, f_cum)


def _dsa_kernel(qi_ref, k0_ref, k1_ref, w_ref, qa_ref, ka_ref, va_ref, dc_ref, o_ref,
                key_ref, mb_ref, lg_ref, kb_ref, *, t_group, sk, topk):
    tq = qa_ref.shape[0]
    t0 = t_group + pl.program_id(1) * tq
    shape = (tq, sk)

    w_lane = lax.broadcasted_iota(jnp.int32, (tq, LANE), 1)
    mb_ref[...] = jnp.zeros(shape, F32)

    kb_ref[0] = k0_ref[0:sk, :].astype(BF16)
    kb_ref[1] = k1_ref[0:sk, :].astype(BF16)

    def idx_quad(hq, carry):
        w = w_ref[...]
        scores = []
        for t in range(2):
            q2 = qi_ref[:, pl.ds(pl.multiple_of((2 * hq + t) * LANE, LANE), LANE)]
            for par in range(2):
                scores.append(lax.dot_general(q2, kb_ref[par], _NT, preferred_element_type=F32))
        acc = mb_ref[...]
        for n, s in enumerate(scores):
            w_h = jnp.sum(jnp.where(w_lane == 4 * hq + n, w, 0.0), axis=-1, keepdims=True)
            acc = acc + jnp.maximum(s, 0.0) * w_h
        mb_ref[...] = acc
        return carry

    lax.fori_loop(0, IDX_HEADS // 4, idx_quad, 0)

    t_pos = lax.broadcasted_iota(jnp.int32, shape, 0) + t0
    s_pos = lax.broadcasted_iota(jnp.int32, shape, 1)
    causal = s_pos <= t_pos
    score = jnp.where(causal, mb_ref[...], -jnp.inf)
    bits = lax.bitcast_convert_type(score, jnp.int32)
    key_ref[...] = jnp.where(bits < 0, bits ^ jnp.int32(0x7FFFFFFF), bits)

    def bit_step(j, thr):
        cand = thr + lax.shift_left(jnp.int32(1), 31 - j)
        cnt = jnp.sum(jnp.where(key_ref[...] >= cand, 1.0, 0.0), axis=-1, keepdims=True)
        return jnp.where(cnt >= topk, cand, thr)

    thr = lax.fori_loop(0, 32, bit_step, jnp.full((tq, 1), INT_MIN, jnp.int32))

    key = key_ref[...]
    ge = key >= thr
    mb_ref[...] = jnp.where(ge & causal, 0.0, NEG_BIG)
    n_ge = jnp.sum(jnp.where(ge, 1.0, 0.0), axis=-1, keepdims=True)
    tied = jnp.where((n_ge > topk) & (thr > KEY_NEG_INF), 1.0, 0.0)

    @pl.when(jnp.max(tied) > 0.0)
    def _():
        kk = key_ref[...]
        gt = kk > thr
        tie = jnp.where(kk == thr, 1.0, 0.0)
        room = topk - jnp.sum(jnp.where(gt, 1.0, 0.0), axis=-1, keepdims=True)
        x = tie
        sh = 1
        while sh < sk:
            x = x + jnp.where(s_pos >= sh, pltpu.roll(x, sh, 1), 0.0)
            sh *= 2
        take = (tie > 0.0) & (x - tie < room)
        mb_ref[...] = jnp.where((gt | take) & causal, 0.0, NEG_BIG)

    unroll = lg_ref.shape[0]
    tile_rows = dc_ref.shape[1]
    lg_ref[:, :, 0:DSA_LEAD] = jnp.zeros((unroll, tq, DSA_LEAD), F32)

    def head_group(g, carry):
        for u in range(unroll):
            h = g * unroll + u
            hcol = pl.ds(pl.multiple_of(h * HEAD_DIM, HEAD_DIM), HEAD_DIM)
            lg_ref[u, :, DSA_LEAD:] = (
                lax.dot_general(qa_ref[:, hcol], ka_ref[0:sk, :], _NT, preferred_element_type=F32)
                + mb_ref[...])
            for r0 in range(0, tq, tile_rows):
                win = pl.ds(pl.multiple_of(t0 + r0, LANE), DSA_LEAD + tile_rows)
                lg_ref[u, r0:r0 + tile_rows, win] += dc_ref[h]
            lg = lg_ref[u, :, DSA_LEAD:]
            p = jnp.exp2(lg - jnp.max(lg, axis=-1, keepdims=True))
            l = jnp.sum(p, axis=-1, keepdims=True)
            o = jnp.dot(p.astype(BF16), va_ref[0:sk, :], preferred_element_type=F32)
            o_ref[0, :, hcol] = (o / l).astype(o_ref.dtype)
        return carry

    lax.fori_loop(0, N_HEADS // unroll, head_group, 0)


def _dsa_attention(proj, proj_small, bias_tiles, batch, seq):
    grp = DSA_GROUP
    topk = min(TOPK_MAX, seq // 4)
    n_idx = IDX_HEADS * IDX_DIM
    tile_rows = bias_tiles.shape[1]
    outs = []
    for g in range(seq // grp):
        sk = (g + 1) * grp
        tq = grp if sk <= 2 * grp else DSA_TQ
        unroll = 8 if sk <= grp else 4
        assert tq % tile_rows == 0
        per_grp = grp // tq
        nq = seq // tq
        qrow = lambda b, j, g=g, nq=nq, per_grp=per_grp: b * nq + g * per_grp + j
        out = pl.pallas_call(
            functools.partial(_dsa_kernel, t_group=g * grp, sk=sk, topk=topk),
            grid=(batch, per_grp),
            in_specs=[pl.BlockSpec((tq, n_idx), lambda b, j, q=qrow: (q(b, j), OFF_QI // n_idx)),
                      pl.BlockSpec((seq, LANE), lambda b, j: (b, 0)),
                      pl.BlockSpec((seq, LANE), lambda b, j: (b, 1)),
                      pl.BlockSpec((tq, LANE), lambda b, j, q=qrow: (q(b, j), 2)),
                      pl.BlockSpec((tq, W_ATT), lambda b, j, q=qrow: (q(b, j), OFF_QA // W_ATT)),
                      pl.BlockSpec((seq, HEAD_DIM), lambda b, j: (b, OFF_KA // HEAD_DIM)),
                      pl.BlockSpec((seq, HEAD_DIM), lambda b, j: (b, OFF_VA // HEAD_DIM)),
                      pl.BlockSpec(bias_tiles.shape, lambda b, j: (0, 0, 0))],
            out_specs=pl.BlockSpec((1, tq, W_ATT), lambda b, j: (b, j, 0)),
            out_shape=jax.ShapeDtypeStruct((batch, grp, W_ATT), BF16),
            scratch_shapes=[pltpu.VMEM((tq, sk), jnp.int32),
                            pltpu.VMEM((tq, sk), F32),
                            pltpu.VMEM((unroll, tq, DSA_LEAD + sk), F32),
                            pltpu.VMEM((2, sk, LANE), BF16)],
            compiler_params=_params(("arbitrary", "arbitrary"), 48),
            name=f"dsa_attention_k{sk}",
        )(proj, proj_small, proj_small, proj_small, proj, proj, proj, bias_tiles)
        outs.append(out)
    return jnp.concatenate(outs, axis=1).reshape(batch * seq, W_ATT)


_SEG_SIZES = (W_ATT, HEAD_DIM, HEAD_DIM, IDX_HEADS * IDX_DIM, IDX_DIM, IDX_HEADS,
              W_ATT, W_ATT, W_ATT, N_HEADS, D_MODEL, D_MODEL)
(_C_QA, _C_KA, _C_VA, _C_QI, _C_KI, _C_WI, _C_QB, _C_KB, _C_VB, _C_FB, _C_GA, _C_GB) = (
    int(v) for v in np.concatenate([[0], np.cumsum(_SEG_SIZES)[:-1]]))


def _proj_blocks():
    def seg(start, width, is_q=0):
        return [(start + PROJ_BN * t, is_q) for t in range(width // PROJ_BN)]

    blocks = (seg(_C_GA, D_MODEL) + seg(_C_GB, D_MODEL) + seg(_C_QA, W_ATT, 1)
              + seg(_C_QB, W_ATT, 1) + seg(_C_KB, W_ATT) + seg(_C_VB, W_ATT)
              + seg(_C_QI, IDX_HEADS * IDX_DIM))
    assert len(blocks) * PROJ_BN == OFF_KA
    blocks.append((_C_KA, 0))
    assert len(blocks) * PROJ_BN == N_MAIN
    return blocks


def _proj_kernel(src_ref, isq_ref, a_ref, wt_hbm, o_ref, wbuf, wbf_ref, sem):
    bn = o_ref.shape[1]
    scale = jnp.where(isq_ref[pl.program_id(0)] == 1, Q_SCALE, 1.0)

    def compute(ws):
        o_ref[...] = lax.dot_general(a_ref[...], ws[0], _NT,
                                     preferred_element_type=F32).astype(o_ref.dtype)

    _with_stationary_weights(
        compute, lambda jj: [wt_hbm.at[pl.ds(pl.multiple_of(src_ref[jj], 8), bn)]],
        wbuf, wbf_ref, sem, prep=lambda w: w * scale)


def _mixer_proj(h, w_t):
    m, k = h.shape
    bm, bn = 1024, PROJ_BN
    blocks = _proj_blocks()
    src = np.array([b[0] for b in blocks], np.int32)
    assert np.all(src % 8 == 0) and np.all(src + bn <= w_t.shape[0])
    is_q = np.array([b[1] for b in blocks], np.int32)
    nj = len(blocks)
    return pl.pallas_call(
        _proj_kernel,
        grid_spec=pltpu.PrefetchScalarGridSpec(
            num_scalar_prefetch=2,
            grid=(nj, m // bm),
            in_specs=[pl.BlockSpec((bm, k), lambda j, i, *_: (i, 0)), _HBM],
            out_specs=pl.BlockSpec((bm, bn), lambda j, i, *_: (i, j)),
            scratch_shapes=_ws_scratch(1, bn, k)),
        out_shape=jax.ShapeDtypeStruct((m, nj * bn), BF16),
        compiler_params=_params(("arbitrary", "arbitrary"), 48),
        name="mixer_proj",
    )(jnp.asarray(src), jnp.asarray(is_q), h, w_t)


def _small_weight_t(w_t):
    k = w_t.shape[1]
    z = lambda n: jnp.zeros((n, k), w_t.dtype)
    k_i = w_t[_C_KI:_C_KI + IDX_DIM]
    return jnp.concatenate([k_i, z(LANE - IDX_DIM), z(LANE - IDX_DIM), k_i,
                            w_t[_C_WI:_C_WI + IDX_HEADS] * IDX_SCALE, z(LANE - IDX_HEADS),
                            w_t[_C_FB:_C_FB + N_HEADS], z(LANE - N_HEADS)], axis=0).astype(BF16)


def _nt_kernel(a_ref, bt_ref, o_ref):
    o_ref[...] = lax.dot_general(a_ref[...], bt_ref[...], _NT,
                                 preferred_element_type=F32).astype(o_ref.dtype)


def _matmul_nt(a, b_t, bm, out_dtype, vmem_mib, name):
    m, k = a.shape
    n = b_t.shape[0]
    return pl.pallas_call(
        _nt_kernel,
        grid=(m // bm,),
        in_specs=[pl.BlockSpec((bm, k), lambda i: (i, 0)),
                  pl.BlockSpec((n, k), lambda i: (0, 0))],
        out_specs=pl.BlockSpec((bm, n), lambda i: (i, 0)),
        out_shape=jax.ShapeDtypeStruct((m, n), out_dtype),
        compiler_params=_params(("arbitrary",), vmem_mib),
        name=name,
    )(a, b_t)


def _ffn(h, w_in, w_out):
    act, w_out_bf = _ffn_in(h, w_in, w_out)
    return _matmul(act, w_out_bf, 512, 512, BF16, 56, "ffn_out")


def kernel(x, c, w_ada, b_ada, g_ffn1, ffn1_w_in, ffn1_w_out, g_mix, w_in, b_forget, rel_bias,
           w_up_a, w_up_b, w_o, g_ffn2, ffn2_w_in, ffn2_w_out, g_final):
    batch, seq, d = x.shape
    assert w_ada.shape[0] == 1, "single-layer trunk"
    xf = x.reshape(batch * seq, d)

    c_pad = jnp.zeros((16, d), F32).at[:batch].set(c)
    bias_tiles = _bias_tiles(rel_bias, DSA_TQ, DSA_LEAD)
    mod = _ada(c_pad, w_ada[0], b_ada)[:batch]
    sh1, sc1, gt1, sh2, sc2, gt2, sh3, sc3, gt3 = [
        v.reshape(batch, 1, d) for v in jnp.split(mod, N_MOD, axis=-1)]

    (h,) = _norm(xf, g_ffn1[0][None, :], batch=batch, shift=sh1, scale=sc1)
    y = _ffn(h, ffn1_w_in[0], ffn1_w_out[0])

    xf, h = _norm(xf, g_mix[0][None, :], batch=batch, y=y, gate=gt1, y_scale=0.5,
                  shift=sh2, scale=sc2)
    w_t = jnp.swapaxes(w_in, 1, 2)[0]
    proj = _mixer_proj(h, w_t)
    proj_small = _matmul_nt(h, _small_weight_t(w_t), 1024, F32, 40, "mixer_proj_small")
    b_f = jnp.zeros((1, LANE), F32).at[0, :N_HEADS].set(b_forget[0])
    f_cum = _forget_cumsum(proj_small, b_f, batch, seq)
    o_b = _fox_attention(proj, f_cum, batch, seq)
    o_a = _dsa_attention(proj, proj_small, bias_tiles, batch, seq)
    merged = _merge(o_a, o_b, w_up_a[0], w_up_b[0], proj)
    y = _matmul_ws(merged, w_o[0], 1024, 512, BF16, 48, "mixer_out")

    xf, h = _norm(xf, g_ffn2[0][None, :], batch=batch, y=y, gate=gt2, y_scale=1.0,
                  shift=sh3, scale=sc3)
    y = _ffn(h, ffn2_w_in[0], ffn2_w_out[0])
    (out,) = _norm(xf, g_final[None, :], batch=batch, y=y, gate=gt3, y_scale=0.5)
    return out.reshape(batch, seq, d)
```
